```python
import jax, jax.numpy as jnp
from jax import lax
import numpy as np

D_MODEL = 1024
BATCH = 8
SEQ = 4096
DEPTH = 2

CHUNK = 64
N_META = 16

D_CONV = 512
CONV_K = 31
D_POOL = 512
POOL_WINDOWS = (2, 4, 8, 16)
N_POOL_GROUPS = len(POOL_WINDOWS)
POOL_GC = D_POOL // N_POOL_GROUPS
POOL_OUT = D_MODEL // N_POOL_GROUPS
D_SC = 512
SC_K = 3
N_BRANCH = 3
N_PROJ = 2 * D_CONV + D_POOL + 3 * D_SC + N_BRANCH * D_MODEL
SPLITS = tuple(int(s) for s in np.cumsum(
    [D_CONV, D_CONV, D_POOL, D_SC, D_SC, D_SC, D_MODEL, D_MODEL]))

N_GROUPS = 4
EXPERTS_PER_GROUP = 8
N_EXPERTS = N_GROUPS * EXPERTS_PER_GROUP
TOP_K = 2
D_EXPERT = 256
DISPATCH_BLOCK = 128

EPS = 1e-6

kernel_name = "hybrid_conv_pool_shortconv_hiermoe"


def rmsnorm(x, g):
    xf = x.astype(jnp.float32)
    y = xf * lax.rsqrt(jnp.mean(xf * xf, axis=-1, keepdims=True) + EPS)
    return (y * g.astype(jnp.float32)).astype(x.dtype)


def layernorm(x, g, b):
    xf = x.astype(jnp.float32)
    mu = jnp.mean(xf, axis=-1, keepdims=True)
    var = jnp.mean(jnp.square(xf - mu), axis=-1, keepdims=True)
    y = (xf - mu) * lax.rsqrt(var + EPS)
    return (y * g.astype(jnp.float32) + b.astype(jnp.float32)).astype(x.dtype)


def causal_dwconv(v, w):
    k, c = w.shape
    rhs = w[:, None, :].astype(v.dtype)
    return lax.conv_general_dilated(
        v, rhs, window_strides=(1,), padding=[(k - 1, 0)],
        dimension_numbers=("NWC", "WIO", "NWC"), feature_group_count=c)


def pool_mixer(v, pool_w, pool_scale):
    _, L, _ = v.shape
    vf = v.astype(jnp.float32)
    cs = jnp.cumsum(vf, axis=1)
    t = jnp.arange(L)
    outs = []
    for g, w in enumerate(POOL_WINDOWS):
        sl = slice(g * POOL_GC, (g + 1) * POOL_GC)
        csg = cs[..., sl]
        shifted = jnp.pad(csg[:, :L - w], ((0, 0), (w, 0), (0, 0)))
        cnt = jnp.minimum(t + 1, w).astype(jnp.float32)[:, None]
        outs.append((csg - shifted) / cnt - vf[..., sl])
    d = jnp.stack(outs, axis=2).astype(v.dtype)
    y = jnp.einsum("blgc,gcd->blgd", d, pool_w.astype(v.dtype))
    return y.reshape(v.shape[0], L, D_MODEL) * pool_scale.astype(v.dtype)


def hybrid_mixer(h, w_in, conv_w, conv_b, conv_ln_g, conv_ln_b, w_conv_out,
                 pool_w, pool_scale, sc_w, w_sc_out, w_out):
    proj = h @ w_in
    a_lin, a_gate, pb, cb, cc, cx, g_a, g_b, g_c = jnp.split(proj, SPLITS, axis=-1)
    u = a_lin * jax.nn.sigmoid(a_gate)
    u = causal_dwconv(u, conv_w) + conv_b.astype(u.dtype)
    u = layernorm(u, conv_ln_g, conv_ln_b)
    y_a = jax.nn.silu(u) @ w_conv_out
    y_b = pool_mixer(pb, pool_w, pool_scale)
    y_c = (cb * causal_dwconv(cc * cx, sc_w)) @ w_sc_out
    merged = jax.nn.sigmoid(g_a) * y_a + jax.nn.sigmoid(g_b) * y_b + jax.nn.sigmoid(g_c) * y_c
    return merged @ w_out


def hier_moe(h, w_group, b_group, w_router, b_router, w1, w3, w2):
    bsz, L, D = h.shape
    T = bsz * L
    hf = h.reshape(T, D)
    logit_g = (hf @ w_group).astype(jnp.float32) + b_group.astype(jnp.float32)
    p_g = jax.nn.softmax(logit_g, axis=-1)
    g_sel = jnp.argmax(logit_g, axis=-1)
    p_sel = jnp.take_along_axis(p_g, g_sel[:, None], axis=-1)
    logit_e = ((hf @ w_router).astype(jnp.float32) + b_router.astype(jnp.float32))
    logit_e = logit_e.reshape(T, N_GROUPS, EXPERTS_PER_GROUP)
    le = jnp.take_along_axis(logit_e, g_sel[:, None, None], axis=1)[:, 0]
    top_v, top_i = lax.top_k(le, TOP_K)
    w_tok = jax.nn.softmax(top_v, axis=-1) * p_sel
    expert_id = g_sel[:, None] * EXPERTS_PER_GROUP + top_i

    A = T * TOP_K
    flat_e = expert_id.reshape(A).astype(jnp.int32)
    flat_tok = jnp.repeat(jnp.arange(T, dtype=jnp.int32), TOP_K)
    flat_w = w_tok.reshape(A)
    order = jnp.argsort(flat_e)
    e_sorted = flat_e[order]
    counts = jnp.bincount(flat_e, length=N_EXPERTS).astype(jnp.int32)
    starts = jnp.cumsum(counts) - counts
    padded = ((counts + DISPATCH_BLOCK - 1) // DISPATCH_BLOCK) * DISPATCH_BLOCK
    pends = jnp.cumsum(padded)
    pstarts = pends - padded
    slot = pstarts[e_sorted] + (jnp.arange(A, dtype=jnp.int32) - starts[e_sorted])
    n_blocks = -(-A // DISPATCH_BLOCK) + N_EXPERTS
    P = n_blocks * DISPATCH_BLOCK
    buf_tok = jnp.zeros((P,), jnp.int32).at[slot].set(flat_tok[order])
    buf_w = jnp.zeros((P,), jnp.float32).at[slot].set(flat_w[order])
    block_start = jnp.arange(n_blocks, dtype=jnp.int32) * DISPATCH_BLOCK
    block_e = jnp.minimum(jnp.searchsorted(pends, block_start, side="right"), N_EXPERTS - 1)

    def run_block(args):
        tok, e = args
        xb = hf[tok]
        return (jax.nn.silu(xb @ w1[e]) * (xb @ w3[e])) @ w2[e]

    out = lax.map(run_block, (buf_tok.reshape(n_blocks, DISPATCH_BLOCK), block_e))
    out = out.reshape(P, D) * buf_w[:, None].astype(out.dtype)
    y = jax.ops.segment_sum(out, buf_tok, num_segments=T)
    return y.reshape(bsz, L, D)


def setup_inputs(seed: int = 0) -> dict:
    key = jax.random.key(seed)
    ks = jax.random.split(key, 24)

    def nrm(k, shape, scale):
        return jax.random.normal(k, shape, jnp.float32) * scale

    Ld = DEPTH
    return {
        "x": nrm(ks[0], (BATCH, SEQ, D_MODEL), 1.0),
        "meta": nrm(ks[1], (N_META, D_MODEL), 1.0),
        "norm_mix": 1.0 + nrm(ks[2], (Ld, D_MODEL), 0.02),
        "w_in": nrm(ks[3], (Ld, D_MODEL, N_PROJ), D_MODEL ** -0.5),
        "conv_w": nrm(ks[4], (Ld, CONV_K, D_CONV), CONV_K ** -0.5),
        "conv_b": nrm(ks[5], (Ld, D_CONV), 0.01),
        "conv_ln_g": 1.0 + nrm(ks[6], (Ld, D_CONV), 0.02),
        "conv_ln_b": nrm(ks[7], (Ld, D_CONV), 0.01),
        "w_conv_out": nrm(ks[8], (Ld, D_CONV, D_MODEL), D_CONV ** -0.5),
        "pool_w": nrm(ks[9], (Ld, N_POOL_GROUPS, POOL_GC, POOL_OUT), POOL_GC ** -0.5),
        "pool_scale": 1.0 + nrm(ks[10], (Ld, D_MODEL), 0.02),
        "sc_w": nrm(ks[11], (Ld, SC_K, D_SC), SC_K ** -0.5),
        "w_sc_out": nrm(ks[12], (Ld, D_SC, D_MODEL), D_SC ** -0.5),
        "w_out": nrm(ks[13], (Ld, D_MODEL, D_MODEL), D_MODEL ** -0.5),
        "norm_ffn": 1.0 + nrm(ks[14], (Ld, D_MODEL), 0.02),
        "w_group": nrm(ks[15], (Ld, D_MODEL, N_GROUPS), D_MODEL ** -0.5),
        "b_group": nrm(ks[16], (Ld, N_GROUPS), 0.01),
        "w_router": nrm(ks[17], (Ld, D_MODEL, N_EXPERTS), D_MODEL ** -0.5),
        "b_router": nrm(ks[18], (Ld, N_EXPERTS), 0.01),
        "w1": nrm(ks[19], (Ld, N_EXPERTS, D_MODEL, D_EXPERT), D_MODEL ** -0.5),
        "w3": nrm(ks[20], (Ld, N_EXPERTS, D_MODEL, D_EXPERT), D_MODEL ** -0.5),
        "w2": nrm(ks[21], (Ld, N_EXPERTS, D_EXPERT, D_MODEL), D_EXPERT ** -0.5),
        "final_norm": 1.0 + nrm(ks[22], (D_MODEL,), 0.02),
    }


def reference(x, meta, norm_mix, w_in, conv_w, conv_b, conv_ln_g, conv_ln_b, w_conv_out,
              pool_w, pool_scale, sc_w, w_sc_out, w_out, norm_ffn, w_group, b_group,
              w_router, b_router, w1, w3, w2, final_norm):
    bsz = x.shape[0]
    m = jnp.broadcast_to(meta.astype(x.dtype)[None], (bsz, N_META, D_MODEL))
    h = jnp.concatenate([m, x], axis=1)
    for l in range(DEPTH):
        h = h + hybrid_mixer(rmsnorm(h, norm_mix[l]), w_in[l], conv_w[l], conv_b[l],
                             conv_ln_g[l], conv_ln_b[l], w_conv_out[l], pool_w[l],
                             pool_scale[l], sc_w[l], w_sc_out[l], w_out[l])
        h = h + hier_moe(rmsnorm(h, norm_ffn[l]), w_group[l], b_group[l], w_router[l],
                         b_router[l], w1[l], w3[l], w2[l])
    h = rmsnorm(h, final_norm)
    return h[:, N_META:]
```

```python
import functools

import numpy as np
import jax
import jax.numpy as jnp
from jax import lax
from jax.experimental import pallas as pl
from jax.experimental.pallas import tpu as pltpu

D_MODEL = 1024
N_META = 16
D_CONV = 512
CONV_K = 31
D_POOL = 512
POOL_WINDOWS = (2, 4, 8, 16)
POOL_GC = 128
POOL_OUT = 256
D_SC = 512
SC_K = 3
N_PROJ = 6144
N_GROUPS = 4
EPG = 8
N_EXPERTS = 32
D_EXPERT = 256
EPS = 1e-6

C_ALIN, C_AGATE, C_PB, C_CB, C_CC, C_CX, C_GA, C_GB, C_GC = (
    0, 512, 1024, 1536, 2048, 2560, 3072, 4096, 5120)

LANES = 128
N_PAIRS = EPG * (EPG - 1) // 2
N_COMBO = N_GROUPS * N_PAIRS
ROUTE_E0 = 4

HALO_U = 32
HALO_PB = 16
HALO_CX = 8
ROW_CHUNK = 32

EXPERT_BLOCK = 128
VMEM_LIMIT = 52 * 1024 * 1024

_PAIR_A = np.array([a for a in range(EPG) for b in range(a + 1, EPG)], np.int32)
_PAIR_B = np.array([b for a in range(EPG) for b in range(a + 1, EPG)], np.int32)


def _sigmoid(x):
    return 0.5 * jnp.tanh(0.5 * x) + 0.5


def _rms(x, g):
    return x * lax.rsqrt(jnp.mean(x * x, axis=-1, keepdims=True) + EPS) * g


def _dot(a, b):
    return jnp.dot(a, b, preferred_element_type=jnp.float32)


def _mixer_kernel(*refs, tl, pos0, has_moe):
    it = iter(refs)
    h_ref = next(it)
    moe_ref = next(it) if has_moe else None
    (hu_ref, hpb_ref, hcx_ref, nmix_ref, w_in_ref, conv_w_ref, conv_b_ref, ln_g_ref, ln_b_ref,
     w_co_ref, pool_w_ref, pool_s_ref, sc_w_ref, w_so_ref, w_out_ref, nffn_ref, w_r_ref, b_r_ref,
     hout_ref, xn_ref, logit_ref, su_ref, spb_ref, scx_ref,
     u_s, pb_s, cx_s, cb_s, act_a, act_b, act_c) = it

    j = pl.program_id(1)

    @pl.when(j == 0)
    def _():
        u_s[0:HALO_U, :] = hu_ref[...]
        pb_s[0:HALO_PB, :] = hpb_ref[...]
        cx_s[0:HALO_CX, :] = hcx_ref[...]

    hin = h_ref[0]
    if has_moe:
        hin = hin + moe_ref[0]
    xb = _rms(hin, nmix_ref[...]).astype(jnp.bfloat16)

    def proj(c0, width):
        return _dot(xb, w_in_ref[:, c0:c0 + width])

    u_s[HALO_U:HALO_U + tl, :] = proj(C_ALIN, D_CONV) * _sigmoid(proj(C_AGATE, D_CONV))
    pb_s[HALO_PB:HALO_PB + tl, :] = proj(C_PB, D_POOL)
    cx_s[HALO_CX:HALO_CX + tl, :] = proj(C_CC, D_SC) * proj(C_CX, D_SC)
    cb_s[...] = proj(C_CB, D_SC)

    rc = min(ROW_CHUNK, tl)
    row0 = pos0 + j * tl

    for base in range(0, tl, rc):
        acc = jnp.broadcast_to(conv_b_ref[...], (rc, D_CONV))
        for k in range(CONV_K):
            acc = acc + conv_w_ref[k:k + 1, :] * u_s[pl.ds(base + (HALO_U - (CONV_K - 1) + k), rc), :]
        mu = jnp.mean(acc, axis=-1, keepdims=True)
        dev = acc - mu
        var = jnp.mean(dev * dev, axis=-1, keepdims=True)
        yn = dev * lax.rsqrt(var + EPS) * ln_g_ref[...] + ln_b_ref[...]
        act_a[pl.ds(base, rc), :] = (yn * _sigmoid(yn)).astype(jnp.bfloat16)
        for g, w in enumerate(POOL_WINDOWS):
            cols = slice(g * POOL_GC, (g + 1) * POOL_GC)
            v = pb_s[pl.ds(base + HALO_PB, rc), cols]
            s = v
            for i in range(1, w):
                s = s + pb_s[pl.ds(base + (HALO_PB - i), rc), cols]
            if pos0 + 1 >= w:
                d = s * (1.0 / w) - v
            else:
                pos = row0 + base + lax.broadcasted_iota(jnp.int32, (rc, POOL_GC), 0)
                d = s / jnp.minimum(pos + 1, w).astype(jnp.float32) - v
            act_b[pl.ds(base, rc), cols] = d.astype(jnp.bfloat16)
        y = sc_w_ref[0:1, :] * cx_s[pl.ds(base + (HALO_CX - 2), rc), :]
        y = y + sc_w_ref[1:2, :] * cx_s[pl.ds(base + (HALO_CX - 1), rc), :]
        y = y + sc_w_ref[2:3, :] * cx_s[pl.ds(base + HALO_CX, rc), :]
        act_c[pl.ds(base, rc), :] = (cb_s[pl.ds(base, rc), :] * y).astype(jnp.bfloat16)

    new_u = u_s[tl:tl + HALO_U, :]
    new_pb = pb_s[tl:tl + HALO_PB, :]
    new_cx = cx_s[tl:tl + HALO_CX, :]
    u_s[0:HALO_U, :] = new_u
    pb_s[0:HALO_PB, :] = new_pb
    cx_s[0:HALO_CX, :] = new_cx
    su_ref[0] = new_u
    spb_ref[0] = new_pb
    scx_ref[0] = new_cx

    merged = _sigmoid(proj(C_GA, D_MODEL)) * _dot(act_a[...], w_co_ref[...])
    y_b = jnp.concatenate(
        [_dot(act_b[:, g * POOL_GC:(g + 1) * POOL_GC], pool_w_ref[g]) for g in range(len(POOL_WINDOWS))],
        axis=-1) * pool_s_ref[...]
    merged = merged + _sigmoid(proj(C_GB, D_MODEL)) * y_b
    merged = merged + _sigmoid(proj(C_GC, D_MODEL)) * _dot(act_c[...], w_so_ref[...])
    hout = hin + _dot(merged.astype(jnp.bfloat16), w_out_ref[...])
    hout_ref[0] = hout

    xn = _rms(hout, nffn_ref[...])
    xn_ref[0] = xn
    logit_ref[0] = _dot(xn.astype(jnp.bfloat16), w_r_ref[...]) + b_r_ref[...]


def _const_spec(shape):
    zeros = (0,) * len(shape)
    return pl.BlockSpec(shape, lambda b, j: zeros, pipeline_mode=pl.Buffered(1))


def _mixer(h, moe, halo, lw, *, tl, pos0):
    bsz, seq, _ = h.shape
    has_moe = moe is not None
    tile = pl.BlockSpec((1, tl, D_MODEL), lambda b, j: (b, j, 0))
    consts = [halo[0], halo[1], halo[2], lw["norm_mix"], lw["w_in"], lw["conv_w"], lw["conv_b"],
              lw["conv_ln_g"], lw["conv_ln_b"], lw["w_conv_out"], lw["pool_w"], lw["pool_scale"],
              lw["sc_w"], lw["w_sc_out"], lw["w_out"], lw["norm_ffn"], lw["w_route"], lw["b_route"]]
    in_specs = [tile] + ([tile] if has_moe else []) + [_const_spec(c.shape) for c in consts]
    out_shape = (
        jax.ShapeDtypeStruct((bsz, seq, D_MODEL), jnp.float32),
        jax.ShapeDtypeStruct((bsz, seq, D_MODEL), jnp.float32),
        jax.ShapeDtypeStruct((bsz, seq, LANES), jnp.float32),
        jax.ShapeDtypeStruct((bsz, HALO_U, D_CONV), jnp.float32),
        jax.ShapeDtypeStruct((bsz, HALO_PB, D_POOL), jnp.float32),
        jax.ShapeDtypeStruct((bsz, HALO_CX, D_SC), jnp.float32),
    )
    out_specs = (
        tile, tile,
        pl.BlockSpec((1, tl, LANES), lambda b, j: (b, j, 0)),
        pl.BlockSpec((1, HALO_U, D_CONV), lambda b, j: (b, 0, 0)),
        pl.BlockSpec((1, HALO_PB, D_POOL), lambda b, j: (b, 0, 0)),
        pl.BlockSpec((1, HALO_CX, D_SC), lambda b, j: (b, 0, 0)),
    )
    scratch = [
        pltpu.VMEM((HALO_U + tl, D_CONV), jnp.float32),
        pltpu.VMEM((HALO_PB + tl, D_POOL), jnp.float32),
        pltpu.VMEM((HALO_CX + tl, D_SC), jnp.float32),
        pltpu.VMEM((tl, D_SC), jnp.float32),
        pltpu.VMEM((tl, D_CONV), jnp.bfloat16),
        pltpu.VMEM((tl, D_POOL), jnp.bfloat16),
        pltpu.VMEM((tl, D_SC), jnp.bfloat16),
    ]
    args = [h] + ([moe] if has_moe else []) + consts
    return pl.pallas_call(
        functools.partial(_mixer_kernel, tl=tl, pos0=pos0, has_moe=has_moe),
        grid=(bsz, seq // tl),
        in_specs=in_specs,
        out_specs=out_specs,
        out_shape=out_shape,
        scratch_shapes=scratch,
        compiler_params=pltpu.CompilerParams(
            dimension_semantics=("arbitrary", "arbitrary"), vmem_limit_bytes=VMEM_LIMIT),
        name="mixer",
    )(*args)


def _route_kernel(logit_ref, tri_ref, info_i_ref, info_f_ref, cnt_ref, carry):
    i = pl.program_id(0)

    @pl.when(i == 0)
    def _():
        carry[...] = jnp.zeros_like(carry)

    logits = logit_ref[...]
    tt = logits.shape[0]
    lane = lax.broadcasted_iota(jnp.int32, (tt, LANES), 1)
    neg = jnp.float32(-jnp.inf)

    def first_max(vals):
        m = jnp.max(vals, axis=-1, keepdims=True)
        idx = jnp.min(jnp.where(vals == m, lane, LANES), axis=-1, keepdims=True)
        return m, idx

    is_g = lane < N_GROUPS
    lg = jnp.where(is_g, logits, neg)
    mg, g_sel = first_max(lg)
    p_sel = 1.0 / jnp.sum(jnp.where(is_g, jnp.exp(lg - mg), 0.0), axis=-1, keepdims=True)

    lo = ROUTE_E0 + EPG * g_sel
    le = jnp.where((lane >= lo) & (lane < lo + EPG), logits, neg)
    v1, i1 = first_max(le)
    v2, i2 = first_max(jnp.where(lane == i1, neg, le))
    t = jnp.exp(v2 - v1)
    w1 = p_sel / (1.0 + t)
    w2 = p_sel * t / (1.0 + t)

    e1 = i1 - lo
    e2 = i2 - lo
    a = jnp.minimum(e1, e2)
    b = jnp.maximum(e1, e2)
    wa = jnp.where(e1 < e2, w1, w2)
    wb = jnp.where(e1 < e2, w2, w1)
    pair = a * (EPG - 1) - ((a * (a - 1)) >> 1) + (b - a - 1)
    combo = g_sel * N_PAIRS + pair

    onehot = lane == combo
    prefix = _dot(tri_ref[...], onehot.astype(jnp.bfloat16))
    rank = jnp.sum(jnp.where(onehot, prefix + carry[...], 0.0), axis=-1, keepdims=True) - 1.0
    carry[...] = carry[...] + prefix[tt - 1:tt, :]

    info_i_ref[...] = jnp.where(lane == 0, combo, jnp.where(lane == 1, rank.astype(jnp.int32), 0))
    info_f_ref[...] = jnp.where(lane == 0, wa, jnp.where(lane == 1, wb, 0.0))
    cnt_ref[...] = jnp.broadcast_to(carry[...], cnt_ref.shape)


def _route(logits, tt):
    n_tok = logits.shape[0]
    tri = jnp.tril(jnp.ones((tt, tt), jnp.bfloat16))
    info_i, info_f, cnt = pl.pallas_call(
        _route_kernel,
        grid=(n_tok // tt,),
        in_specs=[pl.BlockSpec((tt, LANES), lambda i: (i, 0)),
                  pl.BlockSpec((tt, tt), lambda i: (0, 0))],
        out_specs=(pl.BlockSpec((tt, LANES), lambda i: (i, 0)),
                   pl.BlockSpec((tt, LANES), lambda i: (i, 0)),
                   pl.BlockSpec((8, LANES), lambda i: (0, 0))),
        out_shape=(jax.ShapeDtypeStruct((n_tok, LANES), jnp.int32),
                   jax.ShapeDtypeStruct((n_tok, LANES), jnp.float32),
                   jax.ShapeDtypeStruct((8, LANES), jnp.float32)),
        scratch_shapes=[pltpu.VMEM((1, LANES), jnp.float32)],
        compiler_params=pltpu.CompilerParams(dimension_semantics=("arbitrary",)),
        name="route",
    )(logits, tri)
    counts = cnt[0, :N_COMBO].astype(jnp.int32)
    return info_i[:, 0], info_i[:, 1], info_f[:, 0], info_f[:, 1], counts


def _experts_kernel(tok_ref, be1_ref, be2_ref, nused_ref, xn_hbm, ws_ref,
                    w1a_ref, w3a_ref, w2a_ref, w1b_ref, w3b_ref, w2b_ref,
                    out_ref, xbuf, sem, *, blk):
    i = pl.program_id(0)
    nused = nused_ref[0]

    def row_copy(block, r, slot):
        tok = tok_ref[block * blk + r]
        return pltpu.make_async_copy(
            xn_hbm.at[pl.ds(tok, 1), :], xbuf.at[slot, pl.ds(r, 1), :], sem.at[slot])

    def start_block(block, slot):
        def body(r, c):
            row_copy(block, r, slot).start()
            return c
        lax.fori_loop(0, blk, body, 0)

    def wait_block(block, slot):
        def body(r, c):
            row_copy(block, r, slot).wait()
            return c
        lax.fori_loop(0, blk, body, 0)

    @pl.when((i == 0) & (nused > 0))
    def _():
        start_block(0, 0)

    @pl.when(i + 1 < nused)
    def _():
        start_block(i + 1, (i + 1) % 2)

    @pl.when(i < nused)
    def _():
        slot = i % 2
        wait_block(i, slot)
        x = xbuf[slot].astype(jnp.bfloat16)
        ws = ws_ref[0]

        def expert(w1_ref, w3_ref, w2_ref):
            hid = jax.nn.silu(_dot(x, w1_ref[0])) * _dot(x, w3_ref[0])
            return _dot(hid.astype(jnp.bfloat16), w2_ref[0])

        out_ref[...] = (expert(w1a_ref, w3a_ref, w2a_ref) * ws[:, 0:1]
                        + expert(w1b_ref, w3b_ref, w2b_ref) * ws[:, 1:2])

    @pl.when(i >= nused)
    def _():
        out_ref[...] = jnp.zeros_like(out_ref)


def _experts(xn, tok, ws, be1, be2, nused, lw, *, blk):
    n_slots = tok.shape[0]
    n_blocks = n_slots // blk

    def wspec(which):
        def index_map(i, tok_r, be1_r, be2_r, nu_r):
            e = (be1_r, be2_r)[which][i]
            return (e, 0, 0)
        return index_map

    up = (1, D_MODEL, D_EXPERT)
    down = (1, D_EXPERT, D_MODEL)
    grid_spec = pltpu.PrefetchScalarGridSpec(
        num_scalar_prefetch=4,
        grid=(n_blocks,),
        in_specs=[
            pl.BlockSpec(memory_space=pl.ANY),
            pl.BlockSpec((1, blk, 2), lambda i, *_: (i, 0, 0)),
            pl.BlockSpec(up, wspec(0)), pl.BlockSpec(up, wspec(0)), pl.BlockSpec(down, wspec(0)),
            pl.BlockSpec(up, wspec(1)), pl.BlockSpec(up, wspec(1)), pl.BlockSpec(down, wspec(1)),
        ],
        out_specs=pl.BlockSpec((blk, D_MODEL), lambda i, *_: (i, 0)),
        scratch_shapes=[pltpu.VMEM((2, blk, D_MODEL), jnp.float32),
                        pltpu.SemaphoreType.DMA((2,))],
    )
    return pl.pallas_call(
        functools.partial(_experts_kernel, blk=blk),
        grid_spec=grid_spec,
        out_shape=jax.ShapeDtypeStruct((n_slots, D_MODEL), jnp.float32),
        compiler_params=pltpu.CompilerParams(
            dimension_semantics=("arbitrary",), vmem_limit_bytes=VMEM_LIMIT),
        name="experts",
    )(tok, be1, be2, nused, xn, ws.reshape(n_blocks, blk, 2),
      lw["w1"], lw["w3"], lw["w2"], lw["w1"], lw["w3"], lw["w2"])


def _gather_kernel(idx_ref, src_hbm, out_ref, sem, *, rows):
    i = pl.program_id(0)

    def row_copy(r):
        return pltpu.make_async_copy(
            src_hbm.at[pl.ds(idx_ref[i * rows + r], 1), :], out_ref.at[pl.ds(r, 1), :], sem.at[0])

    def start(r, c):
        row_copy(r).start()
        return c

    def wait(r, c):
        row_copy(r).wait()
        return c

    lax.fori_loop(0, rows, start, 0)
    lax.fori_loop(0, rows, wait, 0)


def _gather_rows(src, idx, rows):
    n = idx.shape[0]
    grid_spec = pltpu.PrefetchScalarGridSpec(
        num_scalar_prefetch=1,
        grid=(n // rows,),
        in_specs=[pl.BlockSpec(memory_space=pl.ANY)],
        out_specs=pl.BlockSpec((rows, src.shape[1]), lambda i, *_: (i, 0)),
        scratch_shapes=[pltpu.SemaphoreType.DMA((1,))],
    )
    return pl.pallas_call(
        functools.partial(_gather_kernel, rows=rows),
        grid_spec=grid_spec,
        out_shape=jax.ShapeDtypeStruct((n, src.shape[1]), src.dtype),
        compiler_params=pltpu.CompilerParams(dimension_semantics=("arbitrary",)),
        name="gather_rows",
    )(idx, src)


def _final_kernel(h_ref, moe_ref, g_ref, out_ref):
    out_ref[...] = _rms(h_ref[...] + moe_ref[...], g_ref[...])


def _final_norm(h, moe, g, rows):
    n = h.shape[0]
    tile = pl.BlockSpec((rows, D_MODEL), lambda i: (i, 0))
    return pl.pallas_call(
        _final_kernel,
        grid=(n // rows,),
        in_specs=[tile, tile, pl.BlockSpec((1, D_MODEL), lambda i: (0, 0))],
        out_specs=tile,
        out_shape=jax.ShapeDtypeStruct((n, D_MODEL), jnp.float32),
        compiler_params=pltpu.CompilerParams(dimension_semantics=("arbitrary",)),
        name="final_norm",
    )(h, moe, g)


def _moe(xn, logits, lw, *, tt, blk, gather_rows):
    n_tok = xn.shape[0]
    combo, rank, wa, wb, counts = _route(logits, tt)
    n_blocks = n_tok // blk + min(n_tok, N_COMBO)
    n_slots = n_blocks * blk
    padded = ((counts + blk - 1) // blk) * blk
    pends = jnp.cumsum(padded)
    pstart = pends - padded
    slot = pstart[combo] + rank
    nused = (pends[-1:] // blk).astype(jnp.int32)
    block_c = jnp.minimum(
        jnp.searchsorted(pends, jnp.arange(n_blocks, dtype=jnp.int32) * blk, side="right"),
        N_COMBO - 1).astype(jnp.int32)
    grp = block_c // N_PAIRS
    pair = block_c % N_PAIRS
    be1 = (grp * EPG + jnp.asarray(_PAIR_A)[pair]).astype(jnp.int32)
    be2 = (grp * EPG + jnp.asarray(_PAIR_B)[pair]).astype(jnp.int32)
    tok = jnp.zeros((n_slots,), jnp.int32).at[slot].set(jnp.arange(n_tok, dtype=jnp.int32))
    ws = jnp.stack([wa, wb], axis=-1)[tok]
    out = _experts(xn, tok, ws, be1, be2, nused, lw, blk=blk)
    return _gather_rows(out, slot, gather_rows)


def _layer_weights(p, l):
    bf = jnp.bfloat16
    w_route = jnp.zeros((D_MODEL, LANES), jnp.float32)
    w_route = w_route.at[:, :N_GROUPS].set(p["w_group"][l])
    w_route = w_route.at[:, ROUTE_E0:ROUTE_E0 + N_EXPERTS].set(p["w_router"][l])
    b_route = jnp.zeros((1, LANES), jnp.float32)
    b_route = b_route.at[0, :N_GROUPS].set(p["b_group"][l])
    b_route = b_route.at[0, ROUTE_E0:ROUTE_E0 + N_EXPERTS].set(p["b_router"][l])
    return {
        "norm_mix": p["norm_mix"][l][None, :],
        "w_in": p["w_in"][l].astype(bf),
        "conv_w": p["conv_w"][l],
        "conv_b": p["conv_b"][l][None, :],
        "conv_ln_g": p["conv_ln_g"][l][None, :],
        "conv_ln_b": p["conv_ln_b"][l][None, :],
        "w_conv_out": p["w_conv_out"][l].astype(bf),
        "pool_w": p["pool_w"][l].astype(bf),
        "pool_scale": p["pool_scale"][l][None, :],
        "sc_w": p["sc_w"][l],
        "w_sc_out": p["w_sc_out"][l].astype(bf),
        "w_out": p["w_out"][l].astype(bf),
        "norm_ffn": p["norm_ffn"][l][None, :],
        "w_route": w_route.astype(bf),
        "b_route": b_route,
        "w1": p["w1"][l].astype(bf),
        "w3": p["w3"][l].astype(bf),
        "w2": p["w2"][l].astype(bf),
    }


def kernel(x, meta, norm_mix, w_in, conv_w, conv_b, conv_ln_g, conv_ln_b, w_conv_out, pool_w,
           pool_scale, sc_w, w_sc_out, w_out, norm_ffn, w_group, b_group, w_router, b_router,
           w1, w3, w2, final_norm):
    p = dict(norm_mix=norm_mix, w_in=w_in, conv_w=conv_w, conv_b=conv_b, conv_ln_g=conv_ln_g,
             conv_ln_b=conv_ln_b, w_conv_out=w_conv_out, pool_w=pool_w, pool_scale=pool_scale,
             sc_w=sc_w, w_sc_out=w_sc_out, w_out=w_out, norm_ffn=norm_ffn, w_group=w_group,
             b_group=b_group, w_router=w_router, b_router=b_router, w1=w1, w3=w3, w2=w2)
    bsz, seq, _ = x.shape
    n_tok = bsz * seq
    lws = [_layer_weights(p, l) for l in range(norm_mix.shape[0])]
    zero_halo = (jnp.zeros((HALO_U, D_CONV), jnp.float32),
                 jnp.zeros((HALO_PB, D_POOL), jnp.float32),
                 jnp.zeros((HALO_CX, D_SC), jnp.float32))

    halos = []
    hm = meta[None]
    moe_m = None
    for l, lw in enumerate(lws):
        hm, xn_m, logit_m, su, spb, scx = _mixer(hm, moe_m, zero_halo, lw, tl=N_META, pos0=0)
        halos.append((su[0], spb[0], scx[0]))
        if l + 1 < len(lws):
            moe_m = _moe(xn_m[0], logit_m[0], lw, tt=N_META, blk=EXPERT_BLOCK,
                         gather_rows=N_META)[None]

    h = x
    moe = None
    for l, lw in enumerate(lws):
        h, xn, logits, _, _, _ = _mixer(h, moe, halos[l], lw, tl=256, pos0=N_META)
        moe = _moe(xn.reshape(n_tok, D_MODEL), logits.reshape(n_tok, LANES), lw,
                   tt=512, blk=EXPERT_BLOCK, gather_rows=256).reshape(bsz, seq, D_MODEL)
    out = _final_norm(h.reshape(n_tok, D_MODEL), moe.reshape(n_tok, D_MODEL),
                      final_norm[None, :], 512)
    return out.reshape(bsz, seq, D_MODEL)
```

```python
import functools

import numpy as np
import jax
import jax.numpy as jnp
from jax import lax
from jax.experimental import pallas as pl
from jax.experimental.pallas import tpu as pltpu

D_MODEL = 1024
N_META = 16
D_CONV = 512
CONV_K = 31
D_POOL = 512
POOL_WINDOWS = (2, 4, 8, 16)
POOL_GC = 128
POOL_OUT = 256
D_SC = 512
SC_K = 3
N_GROUPS = 4
EPG = 8
N_EXPERTS = 32
D_EXPERT = 256
EPS = 1e-6

C_ALIN, C_AGATE, C_PB, C_CB, C_CC, C_CX, C_GA, C_GB, C_GC = (
    0, 512, 1024, 1536, 2048, 2560, 3072, 4096, 5120)

LANES = 128
N_PAIRS = EPG * (EPG - 1) // 2
N_COMBO = N_GROUPS * N_PAIRS
ROUTE_E0 = 4
NO_COMBO = LANES - 1

HALO_U = 32
HALO_PB = 16
HALO_CX = 8
ROW_CHUNK = 32

BLK = 128
XS_W = D_MODEL + LANES
LANE_WA, LANE_WB = D_MODEL + 1, D_MODEL + 2
MAIN_TILE = 256
META_TILE = 128
VMEM_LIMIT = 56 * 1024 * 1024

_PAIR_OFF = np.cumsum([0] + [EPG - 1 - a for a in range(EPG - 1)])[:-1].astype(np.int32)


def _sigmoid(x):
    return 0.5 * jnp.tanh(0.5 * x) + 0.5


def _rms(x, g):
    return x * lax.rsqrt(jnp.mean(x * x, axis=-1, keepdims=True) + EPS) * g


def _dot(a, b):
    return jnp.dot(a, b, preferred_element_type=jnp.float32)


def _num_blocks(n_tok):
    return n_tok // BLK + min(n_tok, N_COMBO)


def _route(logits):
    n = logits.shape[0]
    lane = lax.broadcasted_iota(jnp.int32, (n, LANES), 1)
    neg = jnp.float32(-jnp.inf)

    def first_max(vals):
        m = jnp.max(vals, axis=-1, keepdims=True)
        idx = jnp.min(jnp.where(vals == m, lane, LANES), axis=-1, keepdims=True)
        return m, idx

    is_g = lane < N_GROUPS
    lg = jnp.where(is_g, logits, neg)
    mg, g_sel = first_max(lg)
    p_sel = 1.0 / jnp.sum(jnp.where(is_g, jnp.exp(lg - mg), 0.0), axis=-1, keepdims=True)

    lo = ROUTE_E0 + EPG * g_sel
    le = jnp.where((lane >= lo) & (lane < lo + EPG), logits, neg)
    v1, i1 = first_max(le)
    v2, i2 = first_max(jnp.where(lane == i1, neg, le))
    t = jnp.exp(v2 - v1)
    w1 = p_sel / (1.0 + t)
    w2 = p_sel * t / (1.0 + t)

    e1 = i1 - lo
    e2 = i2 - lo
    a = jnp.minimum(e1, e2)
    b = jnp.maximum(e1, e2)
    wa = jnp.where(e1 < e2, w1, w2)
    wb = jnp.where(e1 < e2, w2, w1)
    pair = a * (EPG - 1) - ((a * (a - 1)) >> 1) + (b - a - 1)
    return g_sel * N_PAIRS + pair, wa, wb


def _mixer_kernel(*refs, tl, n_real, pos0, has_moe, n_steps, n_tok, n_blocks):
    it = iter(refs)
    slot_prev = next(it) if has_moe else None
    h_ref = next(it)
    moe_hbm = next(it) if has_moe else None
    (hu_ref, hpb_ref, hcx_ref, nmix_ref, w_in_ref, conv_w_ref, conv_b_ref, ln_g_ref, ln_b_ref,
     w_co_ref, pool_w_ref, pool_s_ref, sc_w_ref, w_so_ref, w_out_ref, nffn_ref, w_r_ref, b_r_ref,
     tri_ref, sup_ref,
     hout_ref, su_ref, spb_ref, scx_ref, xs_hbm, slot_ref, blk_combo, blk_nvalid, nused_ref,
     u_s, pb_s, cx_s, cb_s, act_a, act_b, act_c, stage, meta_v, meta_s, alloc, alloc_iv,
     cnt_s, cur_s, nf_s, sem) = [next(it) for _ in range(45)]
    moe_buf = next(it) if has_moe else None

    j = pl.program_id(1)
    s = pl.program_id(0) * pl.num_programs(1) + j
    dump_row0 = n_blocks * BLK
    SEM_SCATTER, SEM_META, SEM_FLUSH, SEM_GATHER = 0, 1, 2, 3
    ROW_CNT, ROW_CUR, ROW_NF = 0, 1, 2
    LANE_NNEW = LANES - 1

    def meta_copy():
        return pltpu.make_async_copy(meta_v.at[0], meta_s, sem.at[SEM_META])

    def gather_copy(tile, i, par):
        t = jnp.minimum(tile * tl + i, n_tok - 1)
        return pltpu.make_async_copy(
            moe_hbm.at[pl.ds(slot_prev[t], 1), :], moe_buf.at[par, pl.ds(i, 1), :], sem.at[SEM_GATHER])

    def gather_wait():
        pltpu.make_async_copy(moe_hbm.at[pl.ds(0, tl), :], moe_buf.at[0], sem.at[SEM_GATHER]).wait()

    def scatter_row(i, slot, par):
        pltpu.make_async_copy(stage.at[par, pl.ds(i, 1), :], xs_hbm.at[pl.ds(slot, 1), :],
                              sem.at[SEM_SCATTER]).start()

    def scatter_wait():
        pltpu.make_async_copy(stage.at[0], xs_hbm.at[pl.ds(0, tl), :], sem.at[SEM_SCATTER]).wait()

    def register_new_blocks():
        first = nf_s[0]
        n_new = meta_s[tl + LANE_NNEW]

        def body(o, c):
            blk_combo[first + o] = meta_s[tl + o]
            blk_nvalid[first + o] = BLK
            return c
        lax.fori_loop(0, n_new, body, 0)
        nf_s[0] = first + n_new

    @pl.when(s == 0)
    def _():
        def unused(i, c):
            blk_combo[i] = NO_COMBO
            blk_nvalid[i] = 0
            return c
        lax.fori_loop(0, n_blocks + 1, unused, 0)
        nf_s[0] = 0
        alloc[...] = jnp.zeros_like(alloc)
        meta_v[...] = jnp.zeros_like(meta_v)
        meta_copy().start()
        stage[...] = jnp.zeros_like(stage)
        if has_moe:
            def first(i, c):
                gather_copy(0, i, 0).start()
                return c
            lax.fori_loop(0, tl, first, 0)
            gather_wait()

    @pl.when(j == 0)
    def _():
        u_s[0:HALO_U, :] = hu_ref[...]
        pb_s[0:HALO_PB, :] = hpb_ref[...]
        cx_s[0:HALO_CX, :] = hcx_ref[...]

    meta_copy().wait()
    register_new_blocks()

    prev_par = (s + 1) % 2
    cur_par = s % 2
    for i in range(tl):
        scatter_row(i, jnp.where(s > 0, meta_s[i], dump_row0 + i), prev_par)

    if has_moe:
        hin = h_ref[0] + moe_buf[cur_par]
        for i in range(tl):
            gather_copy(s + 1, i, prev_par).start()
    else:
        hin = h_ref[0]
    xb = _rms(hin, nmix_ref[...]).astype(jnp.bfloat16)

    def proj(c0, width):
        return _dot(xb, w_in_ref[:, c0:c0 + width])

    u_s[HALO_U:HALO_U + tl, :] = proj(C_ALIN, D_CONV) * _sigmoid(proj(C_AGATE, D_CONV))
    pb_s[HALO_PB:HALO_PB + tl, :] = proj(C_PB, D_POOL)
    cx_s[HALO_CX:HALO_CX + tl, :] = proj(C_CC, D_SC) * proj(C_CX, D_SC)
    cb_s[...] = proj(C_CB, D_SC)

    rc = min(ROW_CHUNK, tl)
    row0 = pos0 + j * tl

    for base in range(0, tl, rc):
        acc = jnp.broadcast_to(conv_b_ref[...], (rc, D_CONV))
        for k in range(CONV_K):
            acc = acc + conv_w_ref[k:k + 1, :] * u_s[pl.ds(base + (HALO_U - (CONV_K - 1) + k), rc), :]
        mu = jnp.mean(acc, axis=-1, keepdims=True)
        dev = acc - mu
        var = jnp.mean(dev * dev, axis=-1, keepdims=True)
        yn = dev * lax.rsqrt(var + EPS) * ln_g_ref[...] + ln_b_ref[...]
        act_a[pl.ds(base, rc), :] = (yn * _sigmoid(yn)).astype(jnp.bfloat16)
        for g, w in enumerate(POOL_WINDOWS):
            cols = slice(g * POOL_GC, (g + 1) * POOL_GC)
            v = pb_s[pl.ds(base + HALO_PB, rc), cols]
            sm = v
            for i in range(1, w):
                sm = sm + pb_s[pl.ds(base + (HALO_PB - i), rc), cols]
            if pos0 + 1 >= w:
                d = sm * (1.0 / w) - v
            else:
                pos = row0 + base + lax.broadcasted_iota(jnp.int32, (rc, POOL_GC), 0)
                d = sm / jnp.minimum(pos + 1, w).astype(jnp.float32) - v
            act_b[pl.ds(base, rc), cols] = d.astype(jnp.bfloat16)
        y = sc_w_ref[0:1, :] * cx_s[pl.ds(base + (HALO_CX - 2), rc), :]
        y = y + sc_w_ref[1:2, :] * cx_s[pl.ds(base + (HALO_CX - 1), rc), :]
        y = y + sc_w_ref[2:3, :] * cx_s[pl.ds(base + HALO_CX, rc), :]
        act_c[pl.ds(base, rc), :] = (cb_s[pl.ds(base, rc), :] * y).astype(jnp.bfloat16)

    new_u = u_s[n_real:n_real + HALO_U, :]
    new_pb = pb_s[n_real:n_real + HALO_PB, :]
    new_cx = cx_s[n_real:n_real + HALO_CX, :]
    u_s[0:HALO_U, :] = new_u
    pb_s[0:HALO_PB, :] = new_pb
    cx_s[0:HALO_CX, :] = new_cx
    su_ref[0] = new_u
    spb_ref[0] = new_pb
    scx_ref[0] = new_cx

    merged = _sigmoid(proj(C_GA, D_MODEL)) * _dot(act_a[...], w_co_ref[...])
    y_b = jnp.concatenate(
        [_dot(act_b[:, g * POOL_GC:(g + 1) * POOL_GC], pool_w_ref[g]) for g in range(len(POOL_WINDOWS))],
        axis=-1) * pool_s_ref[...]
    merged = merged + _sigmoid(proj(C_GB, D_MODEL)) * y_b
    merged = merged + _sigmoid(proj(C_GC, D_MODEL)) * _dot(act_c[...], w_so_ref[...])
    hout = hin + _dot(merged.astype(jnp.bfloat16), w_out_ref[...])
    hout_ref[0] = hout

    xn = _rms(hout, nffn_ref[...])
    combo, wa, wb = _route(_dot(xn.astype(jnp.bfloat16), w_r_ref[...]) + b_r_ref[...])

    lane = lax.broadcasted_iota(jnp.int32, (tl, LANES), 1)
    onehot = lane == combo
    prefix = _dot(tri_ref[...], onehot.astype(jnp.bfloat16))
    n_tile = prefix[tl - 1:tl, :]
    cnt_v = alloc[ROW_CNT:ROW_CNT + 1, :]
    cur_v = alloc[ROW_CUR:ROW_CUR + 1, :]
    nf_v = alloc[ROW_NF:ROW_NF + 1, :]
    inv_blk = 1.0 / BLK

    def pick(vec):
        return jnp.sum(jnp.where(onehot, vec, 0.0), axis=-1, keepdims=True)

    rank = pick(prefix + cnt_v) - 1.0
    full_before = jnp.floor((cnt_v + (BLK - 1)) * inv_blk)
    n_new = jnp.floor((cnt_v + n_tile + (BLK - 1)) * inv_blk) - full_before
    new_excl = _dot(jnp.broadcast_to(n_new, (8, LANES)).astype(jnp.bfloat16), sup_ref[...])[0:1, :]
    new_base = nf_v + new_excl
    chunk = jnp.floor(rank * inv_blk)
    blk = jnp.where(chunk < pick(full_before), pick(cur_v), pick(new_base - full_before) + chunk)
    slot = blk * BLK + (rank - chunk * BLK)
    n_new_total = jnp.sum(n_new, axis=-1, keepdims=True)
    alloc[ROW_CNT:ROW_CNT + 1, :] = cnt_v + n_tile
    alloc[ROW_CUR:ROW_CUR + 1, :] = jnp.where(n_new > 0.0, new_base + n_new - 1.0, cur_v)
    alloc[ROW_NF:ROW_NF + 1, :] = nf_v + n_new_total

    sq_sub = lax.broadcasted_iota(jnp.int32, (LANES, LANES), 0)
    sq_lane = lax.broadcasted_iota(jnp.int32, (LANES, LANES), 1)
    new_incl_col = jnp.where(sq_sub == 0, jnp.broadcast_to(new_excl + n_new, (LANES, LANES)), 0.0).T[:, 0:1]
    new_combo = jnp.sum(jnp.where(new_incl_col <= sq_lane.astype(jnp.float32), 1.0, 0.0),
                        axis=0, keepdims=True)
    lane_row = lax.broadcasted_iota(jnp.int32, (1, LANES), 1)
    new_combo = jnp.where(lane_row == LANE_NNEW, n_new_total, new_combo)

    info = jnp.where(lane == 0, slot,
                     jnp.where(lane == LANE_WA - D_MODEL, wa,
                               jnp.where(lane == LANE_WB - D_MODEL, wb, 0.0)))
    scatter_wait()
    if has_moe:
        gather_wait()
    stage[cur_par, :, 0:D_MODEL] = xn
    stage[cur_par, :, D_MODEL:XS_W] = info
    slot_rows = info.T[0:8, :].astype(jnp.int32)
    slot_ref[0] = slot_rows
    meta_v[:, 0:tl] = slot_rows
    meta_v[:, tl:tl + LANES] = jnp.broadcast_to(new_combo, (8, LANES)).astype(jnp.int32)
    meta_copy().start()

    @pl.when(s == n_steps - 1)
    def _():
        meta_copy().wait()
        register_new_blocks()

        def flush(i, c):
            scatter_row(i, meta_s[i], cur_par)
            return c
        lax.fori_loop(0, tl, flush, 0)
        scatter_wait()

        alloc_iv[...] = alloc[...].astype(jnp.int32)
        c0 = pltpu.make_async_copy(alloc_iv.at[ROW_CNT], cnt_s, sem.at[SEM_FLUSH])
        c1 = pltpu.make_async_copy(alloc_iv.at[ROW_CUR], cur_s, sem.at[SEM_FLUSH])
        c0.start()
        c1.start()
        c0.wait()
        c1.wait()

        def close(c, carry):
            n = cnt_s[c]
            bidx = jnp.where(n > 0, cur_s[c], n_blocks)
            blk_nvalid[bidx] = ((n - 1) & (BLK - 1)) + 1
            return carry
        lax.fori_loop(0, LANES, close, 0)
        nused_ref[0] = nf_s[0]


def _const_spec(shape):
    zeros = (0,) * len(shape)
    return pl.BlockSpec(shape, lambda b, j, *_: zeros, pipeline_mode=pl.Buffered(1))


def _mixer(h, moe, halo, lw, *, tl, n_real, pos0):
    bsz, seq, _ = h.shape
    nt = seq // tl
    n_tok = bsz * seq
    n_blocks = _num_blocks(n_tok)
    has_moe = moe is not None
    tile = pl.BlockSpec((1, tl, D_MODEL), lambda b, j, *_: (b, j, 0))
    smem = pl.BlockSpec(memory_space=pltpu.SMEM)
    consts = [halo[0], halo[1], halo[2], lw["norm_mix"], lw["w_in"], lw["conv_w"], lw["conv_b"],
              lw["conv_ln_g"], lw["conv_ln_b"], lw["w_conv_out"], lw["pool_w"], lw["pool_scale"],
              lw["sc_w"], lw["w_sc_out"], lw["w_out"], lw["norm_ffn"], lw["w_route"], lw["b_route"],
              jnp.tril(jnp.ones((tl, tl), jnp.bfloat16)),
              jnp.triu(jnp.ones((LANES, LANES), jnp.bfloat16), 1)]
    in_specs = ([tile] + ([pl.BlockSpec(memory_space=pl.ANY)] if has_moe else [])
                + [_const_spec(c.shape) for c in consts])
    out_shape = (
        jax.ShapeDtypeStruct((bsz, seq, D_MODEL), jnp.float32),
        jax.ShapeDtypeStruct((bsz, HALO_U, D_CONV), jnp.float32),
        jax.ShapeDtypeStruct((bsz, HALO_PB, D_POOL), jnp.float32),
        jax.ShapeDtypeStruct((bsz, HALO_CX, D_SC), jnp.float32),
        jax.ShapeDtypeStruct((n_blocks * BLK + tl, XS_W), jnp.float32),
        jax.ShapeDtypeStruct((bsz * nt, 8, tl), jnp.int32),
        jax.ShapeDtypeStruct((n_blocks + 1,), jnp.int32),
        jax.ShapeDtypeStruct((n_blocks + 1,), jnp.int32),
        jax.ShapeDtypeStruct((1,), jnp.int32),
    )
    out_specs = (
        tile,
        pl.BlockSpec((1, HALO_U, D_CONV), lambda b, j, *_: (b, 0, 0)),
        pl.BlockSpec((1, HALO_PB, D_POOL), lambda b, j, *_: (b, 0, 0)),
        pl.BlockSpec((1, HALO_CX, D_SC), lambda b, j, *_: (b, 0, 0)),
        pl.BlockSpec(memory_space=pl.ANY),
        pl.BlockSpec((1, 8, tl), lambda b, j, *_: (b * nt + j, 0, 0)),
        smem, smem, smem,
    )
    scratch = [
        pltpu.VMEM((HALO_U + tl, D_CONV), jnp.float32),
        pltpu.VMEM((HALO_PB + tl, D_POOL), jnp.float32),
        pltpu.VMEM((HALO_CX + tl, D_SC), jnp.float32),
        pltpu.VMEM((tl, D_SC), jnp.float32),
        pltpu.VMEM((tl, D_CONV), jnp.bfloat16),
        pltpu.VMEM((tl, D_POOL), jnp.bfloat16),
        pltpu.VMEM((tl, D_SC), jnp.bfloat16),
        pltpu.VMEM((2, tl, XS_W), jnp.float32),
        pltpu.VMEM((8, tl + LANES), jnp.int32),
        pltpu.SMEM((tl + LANES,), jnp.int32),
        pltpu.VMEM((8, LANES), jnp.float32),
        pltpu.VMEM((8, LANES), jnp.int32),
        pltpu.SMEM((LANES,), jnp.int32),
        pltpu.SMEM((LANES,), jnp.int32),
        pltpu.SMEM((1,), jnp.int32),
        pltpu.SemaphoreType.DMA((6,)),
    ]
    if has_moe:
        scratch.append(pltpu.VMEM((2, tl, D_MODEL), jnp.float32))
    args = ([moe[0]] if has_moe else []) + [h] + ([moe[1]] if has_moe else []) + consts
    grid_spec = pltpu.PrefetchScalarGridSpec(
        num_scalar_prefetch=1 if has_moe else 0,
        grid=(bsz, nt), in_specs=in_specs, out_specs=out_specs, scratch_shapes=scratch)
    return pl.pallas_call(
        functools.partial(_mixer_kernel, tl=tl, n_real=n_real, pos0=pos0, has_moe=has_moe,
                          n_steps=bsz * nt, n_tok=n_tok, n_blocks=n_blocks),
        grid_spec=grid_spec,
        out_shape=out_shape,
        compiler_params=pltpu.CompilerParams(
            dimension_semantics=("arbitrary", "arbitrary"), vmem_limit_bytes=VMEM_LIMIT),
        name="mixer",
    )(*args)


def _experts_kernel(order_ref, be1_ref, be2_ref, nvalid_ref, nused_ref, xs_ref,
                    w1a_ref, w3a_ref, w2a_ref, w1b_ref, w3b_ref, w2b_ref, out_ref):
    i = pl.program_id(0)

    @pl.when(i < nused_ref[0])
    def _():
        rows = xs_ref[...]
        ok = lax.broadcasted_iota(jnp.int32, (BLK, 1), 0) < nvalid_ref[i]
        x = jnp.where(ok, rows[:, 0:D_MODEL], 0.0).astype(jnp.bfloat16)
        wa = jnp.where(ok, rows[:, LANE_WA:LANE_WA + 1], 0.0)
        wb = jnp.where(ok, rows[:, LANE_WB:LANE_WB + 1], 0.0)

        def expert(w1_ref, w3_ref, w2_ref):
            hid = jax.nn.silu(_dot(x, w1_ref[0])) * _dot(x, w3_ref[0])
            return _dot(hid.astype(jnp.bfloat16), w2_ref[0])

        out_ref[...] = (expert(w1a_ref, w3a_ref, w2a_ref) * wa
                        + expert(w1b_ref, w3b_ref, w2b_ref) * wb)


def _experts(xs, blk_combo, blk_nvalid, nused, lw):
    n_blocks = blk_combo.shape[0] - 1
    ids = jnp.arange(n_blocks, dtype=jnp.int32)
    key = (blk_combo[:n_blocks] << 16) | (ids << 7) | jnp.clip(blk_nvalid[:n_blocks] - 1, 0, BLK - 1)
    key = jnp.sort(key)
    last = lax.dynamic_slice(key, (jnp.maximum(nused[0] - 1, 0),), (1,))
    key = jnp.where(ids < nused[0], key, last)
    combo = key >> 16
    order = (key >> 7) & 511
    nvalid = (key & (BLK - 1)) + 1
    grp = combo // N_PAIRS
    pair = combo % N_PAIRS
    a = jnp.sum(pair[:, None] >= jnp.asarray(_PAIR_OFF)[None, 1:], axis=1).astype(jnp.int32)
    off_a = a * (EPG - 1) - ((a * (a - 1)) >> 1)
    be1 = jnp.minimum(grp * EPG + a, N_EXPERTS - 1).astype(jnp.int32)
    be2 = jnp.minimum(grp * EPG + pair - off_a + a + 1, N_EXPERTS - 1).astype(jnp.int32)

    def wspec(which):
        def index_map(i, order_r, be1_r, be2_r, nv_r, nu_r):
            return ((be1_r, be2_r)[which][i], 0, 0)
        return index_map

    def blk_map(i, order_r, *_):
        return (order_r[i], 0)

    up = (1, D_MODEL, D_EXPERT)
    down = (1, D_EXPERT, D_MODEL)
    grid_spec = pltpu.PrefetchScalarGridSpec(
        num_scalar_prefetch=5,
        grid=(n_blocks,),
        in_specs=[
            pl.BlockSpec((BLK, XS_W), blk_map),
            pl.BlockSpec(up, wspec(0)), pl.BlockSpec(up, wspec(0)), pl.BlockSpec(down, wspec(0)),
            pl.BlockSpec(up, wspec(1)), pl.BlockSpec(up, wspec(1)), pl.BlockSpec(down, wspec(1)),
        ],
        out_specs=pl.BlockSpec((BLK, D_MODEL), blk_map),
    )
    return pl.pallas_call(
        _experts_kernel,
        grid_spec=grid_spec,
        out_shape=jax.ShapeDtypeStruct((n_blocks * BLK, D_MODEL), jnp.float32),
        compiler_params=pltpu.CompilerParams(
            dimension_semantics=("arbitrary",), vmem_limit_bytes=VMEM_LIMIT),
        name="experts",
    )(order.astype(jnp.int32), be1, be2, nvalid.astype(jnp.int32), nused, xs,
      lw["w1"], lw["w3"], lw["w2"], lw["w1"], lw["w3"], lw["w2"])


def _final_kernel(slot_ref, h_ref, moe_hbm, g_ref, out_ref, moe_buf, sem, *, rows, n_steps):
    i = pl.program_id(0)

    def start_tile(tile, par):
        def body(r, c):
            pltpu.make_async_copy(moe_hbm.at[pl.ds(slot_ref[tile * rows + r], 1), :],
                                  moe_buf.at[par, pl.ds(r, 1), :], sem.at[par]).start()
            return c
        lax.fori_loop(0, rows, body, 0)

    @pl.when(i == 0)
    def _():
        start_tile(0, 0)

    @pl.when(i + 1 < n_steps)
    def _():
        start_tile(i + 1, (i + 1) % 2)

    par = i % 2
    pltpu.make_async_copy(moe_hbm.at[pl.ds(0, rows), :], moe_buf.at[par], sem.at[par]).wait()
    out_ref[...] = _rms(h_ref[...] + moe_buf[par], g_ref[...])


def _final_norm(h, slot, moe_rows, g, rows):
    n = h.shape[0]
    tile = pl.BlockSpec((rows, D_MODEL), lambda i, *_: (i, 0))
    grid_spec = pltpu.PrefetchScalarGridSpec(
        num_scalar_prefetch=1,
        grid=(n // rows,),
        in_specs=[tile, pl.BlockSpec(memory_space=pl.ANY),
                  pl.BlockSpec((1, D_MODEL), lambda i, *_: (0, 0))],
        out_specs=tile,
        scratch_shapes=[pltpu.VMEM((2, rows, D_MODEL), jnp.float32), pltpu.SemaphoreType.DMA((2,))],
    )
    return pl.pallas_call(
        functools.partial(_final_kernel, rows=rows, n_steps=n // rows),
        grid_spec=grid_spec,
        out_shape=jax.ShapeDtypeStruct((n, D_MODEL), jnp.float32),
        compiler_params=pltpu.CompilerParams(dimension_semantics=("arbitrary",)),
        name="final_norm",
    )(slot, h, moe_rows, g)


def _layer_weights(p, l):
    bf = jnp.bfloat16
    w_route = jnp.zeros((D_MODEL, LANES), jnp.float32)
    w_route = w_route.at[:, :N_GROUPS].set(p["w_group"][l])
    w_route = w_route.at[:, ROUTE_E0:ROUTE_E0 + N_EXPERTS].set(p["w_router"][l])
    b_route = jnp.zeros((1, LANES), jnp.float32)
    b_route = b_route.at[0, :N_GROUPS].set(p["b_group"][l])
    b_route = b_route.at[0, ROUTE_E0:ROUTE_E0 + N_EXPERTS].set(p["b_router"][l])
    return {
        "norm_mix": p["norm_mix"][l][None, :],
        "w_in": p["w_in"][l].astype(bf),
        "conv_w": p["conv_w"][l],
        "conv_b": p["conv_b"][l][None, :],
        "conv_ln_g": p["conv_ln_g"][l][None, :],
        "conv_ln_b": p["conv_ln_b"][l][None, :],
        "w_conv_out": p["w_conv_out"][l].astype(bf),
        "pool_w": p["pool_w"][l].astype(bf),
        "pool_scale": p["pool_scale"][l][None, :],
        "sc_w": p["sc_w"][l],
        "w_sc_out": p["w_sc_out"][l].astype(bf),
        "w_out": p["w_out"][l].astype(bf),
        "norm_ffn": p["norm_ffn"][l][None, :],
        "w_route": w_route.astype(bf),
        "b_route": b_route,
        "w1": p["w1"][l].astype(bf),
        "w3": p["w3"][l].astype(bf),
        "w2": p["w2"][l].astype(bf),
    }


def _layer(h, moe, halo, lw, *, tl, n_real, pos0):
    h, su, spb, scx, xs, slot, blk_combo, blk_nvalid, nused = _mixer(
        h, moe, halo, lw, tl=tl, n_real=n_real, pos0=pos0)
    out = _experts(xs, blk_combo, blk_nvalid, nused, lw)
    return h, (slot[:, 0, :].reshape(-1), out), (su[0], spb[0], scx[0])


def kernel(x, meta, norm_mix, w_in, conv_w, conv_b, conv_ln_g, conv_ln_b, w_conv_out, pool_w,
           pool_scale, sc_w, w_sc_out, w_out, norm_ffn, w_group, b_group, w_router, b_router,
           w1, w3, w2, final_norm):
    p = dict(norm_mix=norm_mix, w_in=w_in, conv_w=conv_w, conv_b=conv_b, conv_ln_g=conv_ln_g,
             conv_ln_b=conv_ln_b, w_conv_out=w_conv_out, pool_w=pool_w, pool_scale=pool_scale,
             sc_w=sc_w, w_sc_out=w_sc_out, w_out=w_out, norm_ffn=norm_ffn, w_group=w_group,
             b_group=b_group, w_router=w_router, b_router=b_router, w1=w1, w3=w3, w2=w2)
    bsz, seq, _ = x.shape
    n_tok = bsz * seq
    n_layers = norm_mix.shape[0]
    lws = [_layer_weights(p, l) for l in range(n_layers)]
    zero_halo = (jnp.zeros((HALO_U, D_CONV), jnp.float32),
                 jnp.zeros((HALO_PB, D_POOL), jnp.float32),
                 jnp.zeros((HALO_CX, D_SC), jnp.float32))

    halos = []
    hm = jnp.zeros((1, META_TILE, D_MODEL), jnp.float32).at[0, :N_META].set(meta)
    moe_m = None
    for l, lw in enumerate(lws):
        if l + 1 < n_layers:
            hm, moe_m, halo = _layer(hm, moe_m, zero_halo, lw, tl=META_TILE, n_real=N_META, pos0=0)
        else:
            halo = _mixer(hm, moe_m, zero_halo, lw, tl=META_TILE, n_real=N_META, pos0=0)[1:4]
            halo = tuple(a[0] for a in halo)
        halos.append(halo)

    h = x
    moe = None
    for l, lw in enumerate(lws):
        h, moe, _ = _layer(h, moe, halos[l], lw, tl=MAIN_TILE, n_real=MAIN_TILE, pos0=N_META)
    out = _final_norm(h.reshape(n_tok, D_MODEL), moe[0], moe[1], final_norm[None, :], MAIN_TILE)
    return out.reshape(bsz, seq, D_MODEL)
```

```python
import functools

import numpy as np
import jax
import jax.numpy as jnp
from jax import lax
from jax.experimental import pallas as pl
from jax.experimental.pallas import tpu as pltpu

D_MODEL = 1024
N_META = 16
D_CONV = 512
CONV_K = 31
D_POOL = 512
POOL_WINDOWS = (2, 4, 8, 16)
POOL_GC = 128
POOL_OUT = 256
D_SC = 512
SC_K = 3
N_GROUPS = 4
EPG = 8
N_EXPERTS = 32
D_EXPERT = 256
EPS = 1e-6

C_ALIN, C_AGATE, C_PB, C_CB, C_CC, C_CX, C_GA, C_GB, C_GC = (
    0, 512, 1024, 1536, 2048, 2560, 3072, 4096, 5120)

LANES = 128
SUBLANES = 8
N_PAIRS = EPG * (EPG - 1) // 2
N_COMBO = N_GROUPS * N_PAIRS
ROUTE_E0 = 4
NO_COMBO = LANES - 1

HALO_U = 32
HALO_PB = 16
HALO_CX = 8
W_BLOCK = 512
GATE_BLOCK = W_BLOCK
ROW_CHUNK = 32

BLK = 128
XS_W = D_MODEL + LANES
LANE_WA, LANE_WB = D_MODEL + 1, D_MODEL + 2
MAIN_TILE = 256
META_TILE = 128
VMEM_LIMIT = 56 * 1024 * 1024

_PAIR_OFF = np.cumsum([0] + [EPG - 1 - a for a in range(EPG - 1)])[:-1].astype(np.int32)


def _sigmoid(x):
    return 0.5 * jnp.tanh(0.5 * x) + 0.5


def _rms(x, g):
    return x * lax.rsqrt(jnp.mean(x * x, axis=-1, keepdims=True) + EPS) * g


def _dot(a, b):
    return jnp.dot(a, b, preferred_element_type=jnp.float32)


def _num_blocks(n_tok):
    return n_tok // BLK + min(n_tok, N_COMBO)


def _route(logits):
    n = logits.shape[0]
    lane = lax.broadcasted_iota(jnp.int32, (n, LANES), 1)
    neg = jnp.float32(-jnp.inf)

    def first_max(vals):
        m = jnp.max(vals, axis=-1, keepdims=True)
        idx = jnp.min(jnp.where(vals == m, lane, LANES), axis=-1, keepdims=True)
        return m, idx

    is_g = lane < N_GROUPS
    lg = jnp.where(is_g, logits, neg)
    mg, g_sel = first_max(lg)
    p_sel = 1.0 / jnp.sum(jnp.where(is_g, jnp.exp(lg - mg), 0.0), axis=-1, keepdims=True)

    lo = ROUTE_E0 + EPG * g_sel
    le = jnp.where((lane >= lo) & (lane < lo + EPG), logits, neg)
    v1, i1 = first_max(le)
    v2, i2 = first_max(jnp.where(lane == i1, neg, le))
    t = jnp.exp(v2 - v1)
    w1 = p_sel / (1.0 + t)
    w2 = p_sel * t / (1.0 + t)

    e1 = i1 - lo
    e2 = i2 - lo
    a = jnp.minimum(e1, e2)
    b = jnp.maximum(e1, e2)
    wa = jnp.where(e1 < e2, w1, w2)
    wb = jnp.where(e1 < e2, w2, w1)
    pair = a * (EPG - 1) - ((a * (a - 1)) >> 1) + (b - a - 1)
    return g_sel * N_PAIRS + pair, wa, wb


def _mixer_kernel(*refs, tl, n_real, pos0, has_moe, n_steps, n_tok, n_blocks):
    it = iter(refs)
    slot_prev = next(it) if has_moe else None
    h_ref = next(it)
    moe_hbm = next(it) if has_moe else None
    (hu_ref, hpb_ref, hcx_ref, nmix_ref, w_in_ref, conv_w_ref, conv_b_ref, ln_g_ref, ln_b_ref,
     w_co_ref, pool_w_ref, pool_s_ref, sc_w_ref, w_so_ref, w_out_ref, nffn_ref, w_r_ref, b_r_ref,
     tri_ref, sup_ref,
     hout_ref, su_ref, spb_ref, scx_ref, xs_hbm, slot_ref, blk_combo, blk_nvalid, nused_ref,
     u_s, u_sh, pb_s, cx_s, cb_s, gate_s, act_a, act_b, act_c, stage, meta_v, meta_s, alloc, alloc_iv,
     cnt_s, cur_s, nf_s, sem) = [next(it) for _ in range(47)]
    moe_buf = next(it) if has_moe else None

    j = pl.program_id(1)
    s = pl.program_id(0) * pl.num_programs(1) + j
    dump_row0 = n_blocks * BLK
    SEM_SCATTER, SEM_META, SEM_FLUSH, SEM_GATHER = 0, 1, 2, 3
    ROW_CNT, ROW_CUR, ROW_NF = 0, 1, 2
    LANE_NNEW = LANES - 1

    def meta_copy():
        return pltpu.make_async_copy(meta_v.at[0], meta_s, sem.at[SEM_META])

    def gather_copy(tile, i, par):
        t = jnp.minimum(tile * tl + i, n_tok - 1)
        return pltpu.make_async_copy(
            moe_hbm.at[pl.ds(slot_prev[t], 1), :], moe_buf.at[par, pl.ds(i, 1), :], sem.at[SEM_GATHER])

    def gather_wait():
        pltpu.make_async_copy(moe_hbm.at[pl.ds(0, tl), :], moe_buf.at[0], sem.at[SEM_GATHER]).wait()

    def scatter_row(i, slot, par):
        pltpu.make_async_copy(stage.at[par, pl.ds(i, 1), :], xs_hbm.at[pl.ds(slot, 1), :],
                              sem.at[SEM_SCATTER]).start()

    def scatter_wait():
        pltpu.make_async_copy(stage.at[0], xs_hbm.at[pl.ds(0, tl), :], sem.at[SEM_SCATTER]).wait()

    def register_new_blocks():
        first = nf_s[0]
        n_new = meta_s[tl + LANE_NNEW]

        def body(o, c):
            blk_combo[first + o] = meta_s[tl + o]
            blk_nvalid[first + o] = BLK
            return c
        lax.fori_loop(0, n_new, body, 0)
        nf_s[0] = first + n_new

    @pl.when(s == 0)
    def _():
        def unused(i, c):
            blk_combo[i] = NO_COMBO
            blk_nvalid[i] = 0
            return c
        lax.fori_loop(0, n_blocks + 1, unused, 0)
        nf_s[0] = 0
        alloc[...] = jnp.zeros_like(alloc)
        meta_v[...] = jnp.zeros_like(meta_v)
        meta_copy().start()
        stage[...] = jnp.zeros_like(stage)
        u_s[HALO_U + tl:HALO_U + tl + SUBLANES, :] = jnp.zeros((SUBLANES, D_CONV), jnp.float32)
        if has_moe:
            def first(i, c):
                gather_copy(0, i, 0).start()
                return c
            lax.fori_loop(0, tl, first, 0)
            gather_wait()

    @pl.when(j == 0)
    def _():
        u_s[0:HALO_U, :] = hu_ref[...]
        pb_s[0:HALO_PB, :] = hpb_ref[...]
        cx_s[0:HALO_CX, :] = hcx_ref[...]

    meta_copy().wait()
    register_new_blocks()

    prev_par = (s + 1) % 2
    cur_par = s % 2
    for i in range(tl):
        scatter_row(i, jnp.where(s > 0, meta_s[i], dump_row0 + i), prev_par)

    if has_moe:
        hin = h_ref[0] + moe_buf[cur_par]
        for i in range(tl):
            gather_copy(s + 1, i, prev_par).start()
    else:
        hin = h_ref[0]
    xb = _rms(hin, nmix_ref[...]).astype(jnp.bfloat16)

    def proj(c0):
        return _dot(xb, w_in_ref[c0 // W_BLOCK])

    u_s[HALO_U:HALO_U + tl, :] = proj(C_ALIN) * (jnp.tanh(proj(C_AGATE)) + 1.0)
    pb_s[HALO_PB:HALO_PB + tl, :] = proj(C_PB)
    cx_s[HALO_CX:HALO_CX + tl, :] = proj(C_CC) * proj(C_CX)
    cb_s[...] = proj(C_CB)
    for lo in range(1, SUBLANES):
        u_sh[lo - 1] = u_s[lo:lo + HALO_U + tl, :]

    rc = min(ROW_CHUNK, tl)
    row0 = pos0 + j * tl

    n_chunks = tl // rc
    n_gate_blocks = 3 * D_MODEL // GATE_BLOCK

    for ci, base in enumerate(range(0, tl, rc)):
        for gb in range(ci * n_gate_blocks // n_chunks, (ci + 1) * n_gate_blocks // n_chunks):
            c0 = gb * GATE_BLOCK
            gate_s[:, c0:c0 + GATE_BLOCK] = jnp.tanh(proj(C_GA + c0)) + 1.0
        acc = jnp.broadcast_to(conv_b_ref[...], (rc, D_CONV))
        for k in range(CONV_K):
            hi, lo = divmod(HALO_U - (CONV_K - 1) + k, SUBLANES)
            src = u_s if lo == 0 else u_sh.at[lo - 1]
            w_k = jnp.concatenate([conv_w_ref[k]] * (rc // SUBLANES), axis=0)
            acc = acc + w_k * src[pl.ds(base + SUBLANES * hi, rc), :]
        mu = jnp.mean(acc, axis=-1, keepdims=True)
        dev = acc - mu
        var = jnp.mean(dev * dev, axis=-1, keepdims=True)
        yn = dev * lax.rsqrt(var + EPS) * ln_g_ref[...] + ln_b_ref[...]
        act_a[pl.ds(base, rc), :] = (yn * _sigmoid(yn)).astype(jnp.bfloat16)
        for g, w in enumerate(POOL_WINDOWS):
            cols = slice(g * POOL_GC, (g + 1) * POOL_GC)
            v = pb_s[pl.ds(base + HALO_PB, rc), cols]
            sm = v
            for i in range(1, w):
                sm = sm + pb_s[pl.ds(base + (HALO_PB - i), rc), cols]
            if pos0 + 1 >= w:
                d = sm * (1.0 / w) - v
            else:
                pos = row0 + base + lax.broadcasted_iota(jnp.int32, (rc, POOL_GC), 0)
                d = sm / jnp.minimum(pos + 1, w).astype(jnp.float32) - v
            act_b[pl.ds(base, rc), cols] = d.astype(jnp.bfloat16)
        y = sc_w_ref[0:1, :] * cx_s[pl.ds(base + (HALO_CX - 2), rc), :]
        y = y + sc_w_ref[1:2, :] * cx_s[pl.ds(base + (HALO_CX - 1), rc), :]
        y = y + sc_w_ref[2:3, :] * cx_s[pl.ds(base + HALO_CX, rc), :]
        act_c[pl.ds(base, rc), :] = (cb_s[pl.ds(base, rc), :] * y).astype(jnp.bfloat16)

    new_u = u_s[n_real:n_real + HALO_U, :]
    new_pb = pb_s[n_real:n_real + HALO_PB, :]
    new_cx = cx_s[n_real:n_real + HALO_CX, :]
    u_s[0:HALO_U, :] = new_u
    pb_s[0:HALO_PB, :] = new_pb
    cx_s[0:HALO_CX, :] = new_cx
    su_ref[0] = new_u
    spb_ref[0] = new_pb
    scx_ref[0] = new_cx

    n_wb = D_MODEL // W_BLOCK
    groups_per_wb = W_BLOCK // POOL_OUT
    merged = []
    for nb in range(n_wb):
        cols = slice(nb * W_BLOCK, (nb + 1) * W_BLOCK)
        y_b = jnp.concatenate(
            [_dot(act_b[:, g * POOL_GC:(g + 1) * POOL_GC], pool_w_ref[g])
             for g in range(nb * groups_per_wb, (nb + 1) * groups_per_wb)], axis=-1) * pool_s_ref[:, cols]
        m = gate_s[:, cols] * _dot(act_a[...], w_co_ref[nb])
        m = m + gate_s[:, D_MODEL + nb * W_BLOCK:D_MODEL + (nb + 1) * W_BLOCK] * y_b
        m = m + gate_s[:, 2 * D_MODEL + nb * W_BLOCK:2 * D_MODEL + (nb + 1) * W_BLOCK] * _dot(
            act_c[...], w_so_ref[nb])
        merged.append(m.astype(jnp.bfloat16))
    hout = hin + jnp.concatenate(
        [sum(_dot(merged[kb], w_out_ref[kb, nb]) for kb in range(n_wb)) for nb in range(n_wb)], axis=-1)
    hout_ref[0] = hout

    xn = _rms(hout, nffn_ref[...])
    combo, wa, wb = _route(_dot(xn.astype(jnp.bfloat16), w_r_ref[...]) + b_r_ref[...])

    lane = lax.broadcasted_iota(jnp.int32, (tl, LANES), 1)
    onehot = lane == combo
    prefix = _dot(tri_ref[...], onehot.astype(jnp.bfloat16))
    n_tile = prefix[tl - 1:tl, :]
    cnt_v = alloc[ROW_CNT:ROW_CNT + 1, :]
    cur_v = alloc[ROW_CUR:ROW_CUR + 1, :]
    nf_v = alloc[ROW_NF:ROW_NF + 1, :]
    inv_blk = 1.0 / BLK

    def pick(vec):
        return jnp.sum(jnp.where(onehot, vec, 0.0), axis=-1, keepdims=True)

    rank = pick(prefix + cnt_v) - 1.0
    full_before = jnp.floor((cnt_v + (BLK - 1)) * inv_blk)
    n_new = jnp.floor((cnt_v + n_tile + (BLK - 1)) * inv_blk) - full_before
    new_excl = _dot(jnp.broadcast_to(n_new, (8, LANES)).astype(jnp.bfloat16), sup_ref[...])[0:1, :]
    new_base = nf_v + new_excl
    chunk = jnp.floor(rank * inv_blk)
    blk = jnp.where(chunk < pick(full_before), pick(cur_v), pick(new_base - full_before) + chunk)
    slot = blk * BLK + (rank - chunk * BLK)
    n_new_total = jnp.sum(n_new, axis=-1, keepdims=True)
    alloc[ROW_CNT:ROW_CNT + 1, :] = cnt_v + n_tile
    alloc[ROW_CUR:ROW_CUR + 1, :] = jnp.where(n_new > 0.0, new_base + n_new - 1.0, cur_v)
    alloc[ROW_NF:ROW_NF + 1, :] = nf_v + n_new_total

    sq_sub = lax.broadcasted_iota(jnp.int32, (LANES, LANES), 0)
    sq_lane = lax.broadcasted_iota(jnp.int32, (LANES, LANES), 1)
    new_incl_col = jnp.where(sq_sub == 0, jnp.broadcast_to(new_excl + n_new, (LANES, LANES)), 0.0).T[:, 0:1]
    new_combo = jnp.sum(jnp.where(new_incl_col <= sq_lane.astype(jnp.float32), 1.0, 0.0),
                        axis=0, keepdims=True)
    lane_row = lax.broadcasted_iota(jnp.int32, (1, LANES), 1)
    new_combo = jnp.where(lane_row == LANE_NNEW, n_new_total, new_combo)

    info = jnp.where(lane == 0, slot,
                     jnp.where(lane == LANE_WA - D_MODEL, wa,
                               jnp.where(lane == LANE_WB - D_MODEL, wb, 0.0)))
    scatter_wait()
    if has_moe:
        gather_wait()
    stage[cur_par, :, 0:D_MODEL] = xn
    stage[cur_par, :, D_MODEL:XS_W] = info
    slot_rows = info.T[0:8, :].astype(jnp.int32)
    slot_ref[0] = slot_rows
    meta_v[:, 0:tl] = slot_rows
    meta_v[:, tl:tl + LANES] = jnp.broadcast_to(new_combo, (8, LANES)).astype(jnp.int32)
    meta_copy().start()

    @pl.when(s == n_steps - 1)
    def _():
        meta_copy().wait()
        register_new_blocks()

        def flush(i, c):
            scatter_row(i, meta_s[i], cur_par)
            return c
        lax.fori_loop(0, tl, flush, 0)
        scatter_wait()

        alloc_iv[...] = alloc[...].astype(jnp.int32)
        c0 = pltpu.make_async_copy(alloc_iv.at[ROW_CNT], cnt_s, sem.at[SEM_FLUSH])
        c1 = pltpu.make_async_copy(alloc_iv.at[ROW_CUR], cur_s, sem.at[SEM_FLUSH])
        c0.start()
        c1.start()
        c0.wait()
        c1.wait()

        def close(c, carry):
            n = cnt_s[c]
            bidx = jnp.where(n > 0, cur_s[c], n_blocks)
            blk_nvalid[bidx] = ((n - 1) & (BLK - 1)) + 1
            return carry
        lax.fori_loop(0, LANES, close, 0)
        nused_ref[0] = nf_s[0]


def _const_spec(shape):
    zeros = (0,) * len(shape)
    return pl.BlockSpec(shape, lambda b, j, *_: zeros, pipeline_mode=pl.Buffered(1))


def _mixer(h, moe, halo, lw, *, tl, n_real, pos0):
    bsz, seq, _ = h.shape
    nt = seq // tl
    n_tok = bsz * seq
    n_blocks = _num_blocks(n_tok)
    has_moe = moe is not None
    tile = pl.BlockSpec((1, tl, D_MODEL), lambda b, j, *_: (b, j, 0))
    smem = pl.BlockSpec(memory_space=pltpu.SMEM)
    consts = [halo[0], halo[1], halo[2], lw["norm_mix"], lw["w_in"], lw["conv_w"], lw["conv_b"],
              lw["conv_ln_g"], lw["conv_ln_b"], lw["w_conv_out"], lw["pool_w"], lw["pool_scale"],
              lw["sc_w"], lw["w_sc_out"], lw["w_out"], lw["norm_ffn"], lw["w_route"], lw["b_route"],
              jnp.tril(jnp.ones((tl, tl), jnp.bfloat16)),
              jnp.triu(jnp.ones((LANES, LANES), jnp.bfloat16), 1)]
    in_specs = ([tile] + ([pl.BlockSpec(memory_space=pl.ANY)] if has_moe else [])
                + [_const_spec(c.shape) for c in consts])
    out_shape = (
        jax.ShapeDtypeStruct((bsz, seq, D_MODEL), jnp.float32),
        jax.ShapeDtypeStruct((bsz, HALO_U, D_CONV), jnp.float32),
        jax.ShapeDtypeStruct((bsz, HALO_PB, D_POOL), jnp.float32),
        jax.ShapeDtypeStruct((bsz, HALO_CX, D_SC), jnp.float32),
        jax.ShapeDtypeStruct((n_blocks * BLK + tl, XS_W), jnp.float32),
        jax.ShapeDtypeStruct((bsz * nt, 8, tl), jnp.int32),
        jax.ShapeDtypeStruct((n_blocks + 1,), jnp.int32),
        jax.ShapeDtypeStruct((n_blocks + 1,), jnp.int32),
        jax.ShapeDtypeStruct((1,), jnp.int32),
    )
    out_specs = (
        tile,
        pl.BlockSpec((1, HALO_U, D_CONV), lambda b, j, *_: (b, 0, 0)),
        pl.BlockSpec((1, HALO_PB, D_POOL), lambda b, j, *_: (b, 0, 0)),
        pl.BlockSpec((1, HALO_CX, D_SC), lambda b, j, *_: (b, 0, 0)),
        pl.BlockSpec(memory_space=pl.ANY),
        pl.BlockSpec((1, 8, tl), lambda b, j, *_: (b * nt + j, 0, 0)),
        smem, smem, smem,
    )
    scratch = [
        pltpu.VMEM((HALO_U + tl + SUBLANES, D_CONV), jnp.float32),
        pltpu.VMEM((SUBLANES - 1, HALO_U + tl, D_CONV), jnp.float32),
        pltpu.VMEM((HALO_PB + tl, D_POOL), jnp.float32),
        pltpu.VMEM((HALO_CX + tl, D_SC), jnp.float32),
        pltpu.VMEM((tl, D_SC), jnp.float32),
        pltpu.VMEM((tl, 3 * D_MODEL), jnp.float32),
        pltpu.VMEM((tl, D_CONV), jnp.bfloat16),
        pltpu.VMEM((tl, D_POOL), jnp.bfloat16),
        pltpu.VMEM((tl, D_SC), jnp.bfloat16),
        pltpu.VMEM((2, tl, XS_W), jnp.float32),
        pltpu.VMEM((8, tl + LANES), jnp.int32),
        pltpu.SMEM((tl + LANES,), jnp.int32),
        pltpu.VMEM((8, LANES), jnp.float32),
        pltpu.VMEM((8, LANES), jnp.int32),
        pltpu.SMEM((LANES,), jnp.int32),
        pltpu.SMEM((LANES,), jnp.int32),
        pltpu.SMEM((1,), jnp.int32),
        pltpu.SemaphoreType.DMA((6,)),
    ]
    if has_moe:
        scratch.append(pltpu.VMEM((2, tl, D_MODEL), jnp.float32))
    args = ([moe[0]] if has_moe else []) + [h] + ([moe[1]] if has_moe else []) + consts
    grid_spec = pltpu.PrefetchScalarGridSpec(
        num_scalar_prefetch=1 if has_moe else 0,
        grid=(bsz, nt), in_specs=in_specs, out_specs=out_specs, scratch_shapes=scratch)
    return pl.pallas_call(
        functools.partial(_mixer_kernel, tl=tl, n_real=n_real, pos0=pos0, has_moe=has_moe,
                          n_steps=bsz * nt, n_tok=n_tok, n_blocks=n_blocks),
        grid_spec=grid_spec,
        out_shape=out_shape,
        compiler_params=pltpu.CompilerParams(
            dimension_semantics=("arbitrary", "arbitrary"), vmem_limit_bytes=VMEM_LIMIT),
        name="mixer",
    )(*args)


def _experts_kernel(order_ref, be1_ref, be2_ref, nvalid_ref, nused_ref, xs_ref,
                    w1a_ref, w3a_ref, w2a_ref, w1b_ref, w3b_ref, w2b_ref, out_ref):
    i = pl.program_id(0)

    @pl.when(i < nused_ref[0])
    def _():
        rows = xs_ref[...]
        ok = lax.broadcasted_iota(jnp.int32, (BLK, 1), 0) < nvalid_ref[i]
        x = jnp.where(ok, rows[:, 0:D_MODEL], 0.0).astype(jnp.bfloat16)
        wa = jnp.where(ok, rows[:, LANE_WA:LANE_WA + 1], 0.0)
        wb = jnp.where(ok, rows[:, LANE_WB:LANE_WB + 1], 0.0)

        def expert(w1_ref, w3_ref, w2_ref):
            hid = jax.nn.silu(_dot(x, w1_ref[0])) * _dot(x, w3_ref[0])
            return _dot(hid.astype(jnp.bfloat16), w2_ref[0])

        out_ref[...] = (expert(w1a_ref, w3a_ref, w2a_ref) * wa
                        + expert(w1b_ref, w3b_ref, w2b_ref) * wb)


def _experts(xs, blk_combo, blk_nvalid, nused, lw):
    n_blocks = blk_combo.shape[0] - 1
    ids = jnp.arange(n_blocks, dtype=jnp.int32)
    key = (blk_combo[:n_blocks] << 16) | (ids << 7) | jnp.clip(blk_nvalid[:n_blocks] - 1, 0, BLK - 1)
    key = jnp.sort(key)
    last = lax.dynamic_slice(key, (jnp.maximum(nused[0] - 1, 0),), (1,))
    key = jnp.where(ids < nused[0], key, last)
    combo = key >> 16
    order = (key >> 7) & 511
    nvalid = (key & (BLK - 1)) + 1
    grp = combo // N_PAIRS
    pair = combo % N_PAIRS
    a = jnp.sum(pair[:, None] >= jnp.asarray(_PAIR_OFF)[None, 1:], axis=1).astype(jnp.int32)
    off_a = a * (EPG - 1) - ((a * (a - 1)) >> 1)
    be1 = jnp.minimum(grp * EPG + a, N_EXPERTS - 1).astype(jnp.int32)
    be2 = jnp.minimum(grp * EPG + pair - off_a + a + 1, N_EXPERTS - 1).astype(jnp.int32)

    def wspec(which):
        def index_map(i, order_r, be1_r, be2_r, nv_r, nu_r):
            return ((be1_r, be2_r)[which][i], 0, 0)
        return index_map

    def blk_map(i, order_r, *_):
        return (order_r[i], 0)

    up = (1, D_MODEL, D_EXPERT)
    down = (1, D_EXPERT, D_MODEL)
    grid_spec = pltpu.PrefetchScalarGridSpec(
        num_scalar_prefetch=5,
        grid=(n_blocks,),
        in_specs=[
            pl.BlockSpec((BLK, XS_W), blk_map),
            pl.BlockSpec(up, wspec(0)), pl.BlockSpec(up, wspec(0)), pl.BlockSpec(down, wspec(0)),
            pl.BlockSpec(up, wspec(1)), pl.BlockSpec(up, wspec(1)), pl.BlockSpec(down, wspec(1)),
        ],
        out_specs=pl.BlockSpec((BLK, D_MODEL), blk_map),
    )
    return pl.pallas_call(
        _experts_kernel,
        grid_spec=grid_spec,
        out_shape=jax.ShapeDtypeStruct((n_blocks * BLK, D_MODEL), jnp.float32),
        compiler_params=pltpu.CompilerParams(
            dimension_semantics=("arbitrary",), vmem_limit_bytes=VMEM_LIMIT),
        name="experts",
    )(order.astype(jnp.int32), be1, be2, nvalid.astype(jnp.int32), nused, xs,
      lw["w1"], lw["w3"], lw["w2"], lw["w1"], lw["w3"], lw["w2"])


def _final_kernel(slot_ref, h_ref, moe_hbm, g_ref, out_ref, moe_buf, sem, *, rows, n_steps):
    i = pl.program_id(0)

    def start_tile(tile, par):
        def body(r, c):
            pltpu.make_async_copy(moe_hbm.at[pl.ds(slot_ref[tile * rows + r], 1), :],
                                  moe_buf.at[par, pl.ds(r, 1), :], sem.at[par]).start()
            return c
        lax.fori_loop(0, rows, body, 0)

    @pl.when(i == 0)
    def _():
        start_tile(0, 0)

    @pl.when(i + 1 < n_steps)
    def _():
        start_tile(i + 1, (i + 1) % 2)

    par = i % 2
    pltpu.make_async_copy(moe_hbm.at[pl.ds(0, rows), :], moe_buf.at[par], sem.at[par]).wait()
    out_ref[...] = _rms(h_ref[...] + moe_buf[par], g_ref[...])


def _final_norm(h, slot, moe_rows, g, rows):
    n = h.shape[0]
    tile = pl.BlockSpec((rows, D_MODEL), lambda i, *_: (i, 0))
    grid_spec = pltpu.PrefetchScalarGridSpec(
        num_scalar_prefetch=1,
        grid=(n // rows,),
        in_specs=[tile, pl.BlockSpec(memory_space=pl.ANY),
                  pl.BlockSpec((1, D_MODEL), lambda i, *_: (0, 0))],
        out_specs=tile,
        scratch_shapes=[pltpu.VMEM((2, rows, D_MODEL), jnp.float32), pltpu.SemaphoreType.DMA((2,))],
    )
    return pl.pallas_call(
        functools.partial(_final_kernel, rows=rows, n_steps=n // rows),
        grid_spec=grid_spec,
        out_shape=jax.ShapeDtypeStruct((n, D_MODEL), jnp.float32),
        compiler_params=pltpu.CompilerParams(dimension_semantics=("arbitrary",)),
        name="final_norm",
    )(slot, h, moe_rows, g)


def _layer_weights(p, l):
    bf = jnp.bfloat16
    w_route = jnp.zeros((D_MODEL, LANES), jnp.float32)
    w_route = w_route.at[:, :N_GROUPS].set(p["w_group"][l])
    w_route = w_route.at[:, ROUTE_E0:ROUTE_E0 + N_EXPERTS].set(p["w_router"][l])
    b_route = jnp.zeros((1, LANES), jnp.float32)
    b_route = b_route.at[0, :N_GROUPS].set(p["b_group"][l])
    b_route = b_route.at[0, ROUTE_E0:ROUTE_E0 + N_EXPERTS].set(p["b_router"][l])

    def col_blocks(w):
        k, n = w.shape
        return w.reshape(k, n // W_BLOCK, W_BLOCK).transpose(1, 0, 2).astype(bf)

    col = jnp.arange(p["w_in"].shape[-1])
    halved = ((col < C_PB) | (col >= C_GA))
    w_in = p["w_in"][l] * jnp.where(halved, 0.5, 1.0)[None, :]
    w_out = (0.5 * p["w_out"][l]).reshape(D_MODEL // W_BLOCK, W_BLOCK, D_MODEL // W_BLOCK, W_BLOCK)
    return {
        "norm_mix": p["norm_mix"][l][None, :],
        "w_in": col_blocks(w_in),
        "conv_w": jnp.broadcast_to(p["conv_w"][l][:, None, :], (CONV_K, SUBLANES, D_CONV)),
        "conv_b": p["conv_b"][l][None, :],
        "conv_ln_g": p["conv_ln_g"][l][None, :],
        "conv_ln_b": p["conv_ln_b"][l][None, :],
        "w_conv_out": col_blocks(p["w_conv_out"][l]),
        "pool_w": p["pool_w"][l].astype(bf),
        "pool_scale": p["pool_scale"][l][None, :],
        "sc_w": p["sc_w"][l],
        "w_sc_out": col_blocks(p["w_sc_out"][l]),
        "w_out": w_out.transpose(0, 2, 1, 3).astype(bf),
        "norm_ffn": p["norm_ffn"][l][None, :],
        "w_route": w_route.astype(bf),
        "b_route": b_route,
        "w1": p["w1"][l].astype(bf),
        "w3": p["w3"][l].astype(bf),
        "w2": p["w2"][l].astype(bf),
    }


def _layer(h, moe, halo, lw, *, tl, n_real, pos0):
    h, su, spb, scx, xs, slot, blk_combo, blk_nvalid, nused = _mixer(
        h, moe, halo, lw, tl=tl, n_real=n_real, pos0=pos0)
    out = _experts(xs, blk_combo, blk_nvalid, nused, lw)
    return h, (slot[:, 0, :].reshape(-1), out), (su[0], spb[0], scx[0])


def kernel(x, meta, norm_mix, w_in, conv_w, conv_b, conv_ln_g, conv_ln_b, w_conv_out, pool_w,
           pool_scale, sc_w, w_sc_out, w_out, norm_ffn, w_group, b_group, w_router, b_router,
           w1, w3, w2, final_norm):
    p = dict(norm_mix=norm_mix, w_in=w_in, conv_w=conv_w, conv_b=conv_b, conv_ln_g=conv_ln_g,
             conv_ln_b=conv_ln_b, w_conv_out=w_conv_out, pool_w=pool_w, pool_scale=pool_scale,
             sc_w=sc_w, w_sc_out=w_sc_out, w_out=w_out, norm_ffn=norm_ffn, w_group=w_group,
             b_group=b_group, w_router=w_router, b_router=b_router, w1=w1, w3=w3, w2=w2)
    bsz, seq, _ = x.shape
    n_tok = bsz * seq
    n_layers = norm_mix.shape[0]
    lws = [_layer_weights(p, l) for l in range(n_layers)]
    zero_halo = (jnp.zeros((HALO_U, D_CONV), jnp.float32),
                 jnp.zeros((HALO_PB, D_POOL), jnp.float32),
                 jnp.zeros((HALO_CX, D_SC), jnp.float32))

    halos = []
    hm = jnp.zeros((1, META_TILE, D_MODEL), jnp.float32).at[0, :N_META].set(meta)
    moe_m = None
    for l, lw in enumerate(lws):
        if l + 1 < n_layers:
            hm, moe_m, halo = _layer(hm, moe_m, zero_halo, lw, tl=META_TILE, n_real=N_META, pos0=0)
        else:
            halo = _mixer(hm, moe_m, zero_halo, lw, tl=META_TILE, n_real=N_META, pos0=0)[1:4]
            halo = tuple(a[0] for a in halo)
        halos.append(halo)

    h = x
    moe = None
    for l, lw in enumerate(lws):
        h, moe, _ = _layer(h, moe, halos[l], lw, tl=MAIN_TILE, n_real=MAIN_TILE, pos0=N_META)
    out = _final_norm(h.reshape(n_tok, D_MODEL), moe[0], moe[1], final_norm[None, :], MAIN_TILE)
    return out.reshape(bsz, seq, D_MODEL)
```

```python
import functools

import numpy as np
import jax
import jax.numpy as jnp
from jax import lax
from jax.experimental import pallas as pl
from jax.experimental.pallas import tpu as pltpu

D_MODEL = 1024
N_META = 16
D_CONV = 512
CONV_K = 31
D_POOL = 512
POOL_WINDOWS = (2, 4, 8, 16)
POOL_GC = 128
POOL_OUT = 256
D_SC = 512
SC_K = 3
N_GROUPS = 4
EPG = 8
N_EXPERTS = 32
D_EXPERT = 256
EPS = 1e-6

C_ALIN, C_AGATE, C_PB, C_CB, C_CC, C_CX, C_GA, C_GB, C_GC = (
    0, 512, 1024, 1536, 2048, 2560, 3072, 4096, 5120)

LANES = 128
SUBLANES = 8
N_PAIRS = EPG * (EPG - 1) // 2
N_COMBO = N_GROUPS * N_PAIRS
ROUTE_E0 = 4
NO_COMBO = LANES - 1

HALO_U = 32
HALO_PB = 16
HALO_CX = 8
W_BLOCK = 512
GATE_BLOCK = W_BLOCK
ROW_CHUNK = 32

BLK = 128
XS_W = D_MODEL + LANES
LANE_WA, LANE_WB = D_MODEL + 1, D_MODEL + 2
MAIN_TILE = 512
META_TILE = 128
VMEM_LIMIT = 56 * 1024 * 1024

_PAIR_OFF = np.cumsum([0] + [EPG - 1 - a for a in range(EPG - 1)])[:-1].astype(np.int32)


def _sigmoid(x):
    return 0.5 * jnp.tanh(0.5 * x) + 0.5


def _rms(x, g):
    return x * lax.rsqrt(jnp.mean(x * x, axis=-1, keepdims=True) + EPS) * g


def _dot(a, b):
    return jnp.dot(a, b, preferred_element_type=jnp.float32)


def _num_blocks(n_tok):
    return n_tok // BLK + min(n_tok, N_COMBO)


def _route(logits):
    n = logits.shape[0]
    lane = lax.broadcasted_iota(jnp.int32, (n, LANES), 1)
    neg = jnp.float32(-jnp.inf)

    def first_max(vals):
        m = jnp.max(vals, axis=-1, keepdims=True)
        idx = jnp.min(jnp.where(vals == m, lane, LANES), axis=-1, keepdims=True)
        return m, idx

    is_g = lane < N_GROUPS
    lg = jnp.where(is_g, logits, neg)
    mg, g_sel = first_max(lg)
    p_sel = 1.0 / jnp.sum(jnp.where(is_g, jnp.exp(lg - mg), 0.0), axis=-1, keepdims=True)

    lo = ROUTE_E0 + EPG * g_sel
    le = jnp.where((lane >= lo) & (lane < lo + EPG), logits, neg)
    v1, i1 = first_max(le)
    v2, i2 = first_max(jnp.where(lane == i1, neg, le))
    t = jnp.exp(v2 - v1)
    w1 = p_sel / (1.0 + t)
    w2 = p_sel * t / (1.0 + t)

    e1 = i1 - lo
    e2 = i2 - lo
    a = jnp.minimum(e1, e2)
    b = jnp.maximum(e1, e2)
    wa = jnp.where(e1 < e2, w1, w2)
    wb = jnp.where(e1 < e2, w2, w1)
    pair = a * (EPG - 1) - ((a * (a - 1)) >> 1) + (b - a - 1)
    return g_sel * N_PAIRS + pair, wa, wb


def _mixer_kernel(*refs, tl, n_real, pos0, has_moe, n_steps, n_tok, n_blocks):
    it = iter(refs)
    slot_prev = next(it) if has_moe else None
    h_ref = next(it)
    moe_hbm = next(it) if has_moe else None
    (hu_ref, hpb_ref, hcx_ref, nmix_ref, w_in_ref, conv_w_ref, conv_b_ref, ln_g_ref, ln_b_ref,
     w_co_ref, pool_w_ref, pool_s_ref, sc_w_ref, w_so_ref, w_out_ref, nffn_ref, w_r_ref, b_r_ref,
     tri_ref, sup_ref,
     hout_ref, su_ref, spb_ref, scx_ref, xs_hbm, slot_ref, blk_combo, blk_nvalid, nused_ref,
     u_s, u_sh, pb_s, cx_s, cb_s, gate_s, act_a, act_b, act_c, stage, meta_v, meta_s, alloc, alloc_iv,
     cnt_s, cur_s, nf_s, sem) = [next(it) for _ in range(47)]
    moe_buf = next(it) if has_moe else None

    j = pl.program_id(1)
    s = pl.program_id(0) * pl.num_programs(1) + j
    dump_row0 = n_blocks * BLK
    SEM_SCATTER, SEM_META, SEM_FLUSH, SEM_GATHER = 0, 1, 2, 3
    ROW_CNT, ROW_CUR, ROW_NF = 0, 1, 2
    LANE_NNEW = LANES - 1

    def meta_copy():
        return pltpu.make_async_copy(meta_v.at[0], meta_s, sem.at[SEM_META])

    def gather_copy(tile, i, par):
        t = jnp.minimum(tile * tl + i, n_tok - 1)
        return pltpu.make_async_copy(
            moe_hbm.at[pl.ds(slot_prev[t], 1), :], moe_buf.at[par, pl.ds(i, 1), :], sem.at[SEM_GATHER])

    def gather_wait():
        pltpu.make_async_copy(moe_hbm.at[pl.ds(0, tl), :], moe_buf.at[0], sem.at[SEM_GATHER]).wait()

    def scatter_row(i, slot, par):
        pltpu.make_async_copy(stage.at[par, pl.ds(i, 1), :], xs_hbm.at[pl.ds(slot, 1), :],
                              sem.at[SEM_SCATTER]).start()

    def scatter_wait():
        pltpu.make_async_copy(stage.at[0], xs_hbm.at[pl.ds(0, tl), :], sem.at[SEM_SCATTER]).wait()

    def register_new_blocks():
        first = nf_s[0]
        n_new = meta_s[tl + LANE_NNEW]

        def body(o, c):
            blk_combo[first + o] = meta_s[tl + o]
            blk_nvalid[first + o] = BLK
            return c
        lax.fori_loop(0, n_new, body, 0)
        nf_s[0] = first + n_new

    @pl.when(s == 0)
    def _():
        def unused(i, c):
            blk_combo[i] = NO_COMBO
            blk_nvalid[i] = 0
            return c
        lax.fori_loop(0, n_blocks + 1, unused, 0)
        nf_s[0] = 0
        alloc[...] = jnp.zeros_like(alloc)
        meta_v[...] = jnp.zeros_like(meta_v)
        meta_copy().start()
        stage[...] = jnp.zeros_like(stage)
        u_s[HALO_U + tl:HALO_U + tl + SUBLANES, :] = jnp.zeros((SUBLANES, D_CONV), jnp.float32)
        if has_moe:
            def first(i, c):
                gather_copy(0, i, 0).start()
                return c
            lax.fori_loop(0, tl, first, 0)
            gather_wait()

    @pl.when(j == 0)
    def _():
        u_s[0:HALO_U, :] = hu_ref[...]
        pb_s[0:HALO_PB, :] = hpb_ref[...]
        cx_s[0:HALO_CX, :] = hcx_ref[...]

    meta_copy().wait()
    register_new_blocks()

    prev_par = (s + 1) % 2
    cur_par = s % 2
    for i in range(tl):
        scatter_row(i, jnp.where(s > 0, meta_s[i], dump_row0 + i), prev_par)

    if has_moe:
        hin = h_ref[0] + moe_buf[cur_par]
        for i in range(tl):
            gather_copy(s + 1, i, prev_par).start()
    else:
        hin = h_ref[0]
    xb = _rms(hin, nmix_ref[...]).astype(jnp.bfloat16)

    def proj(c0):
        return _dot(xb, w_in_ref[c0 // W_BLOCK])

    u_s[HALO_U:HALO_U + tl, :] = proj(C_ALIN) * (jnp.tanh(proj(C_AGATE)) + 1.0)
    pb_s[HALO_PB:HALO_PB + tl, :] = proj(C_PB)
    cx_s[HALO_CX:HALO_CX + tl, :] = proj(C_CC) * proj(C_CX)
    cb_s[...] = proj(C_CB)
    for lo in range(1, SUBLANES):
        u_sh[lo - 1] = u_s[lo:lo + HALO_U + tl, :]

    rc = min(ROW_CHUNK, tl)
    row0 = pos0 + j * tl

    n_chunks = tl // rc
    n_gate_blocks = 3 * D_MODEL // GATE_BLOCK

    for ci, base in enumerate(range(0, tl, rc)):
        for gb in range(ci * n_gate_blocks // n_chunks, (ci + 1) * n_gate_blocks // n_chunks):
            c0 = gb * GATE_BLOCK
            gate_s[:, c0:c0 + GATE_BLOCK] = jnp.tanh(proj(C_GA + c0)) + 1.0
        acc = jnp.broadcast_to(conv_b_ref[...], (rc, D_CONV))
        for k in range(CONV_K):
            hi, lo = divmod(HALO_U - (CONV_K - 1) + k, SUBLANES)
            src = u_s if lo == 0 else u_sh.at[lo - 1]
            w_k = jnp.concatenate([conv_w_ref[k]] * (rc // SUBLANES), axis=0)
            acc = acc + w_k * src[pl.ds(base + SUBLANES * hi, rc), :]
        mu = jnp.mean(acc, axis=-1, keepdims=True)
        dev = acc - mu
        var = jnp.mean(dev * dev, axis=-1, keepdims=True)
        yn = dev * lax.rsqrt(var + EPS) * ln_g_ref[...] + ln_b_ref[...]
        act_a[pl.ds(base, rc), :] = (yn * _sigmoid(yn)).astype(jnp.bfloat16)
        for g, w in enumerate(POOL_WINDOWS):
            cols = slice(g * POOL_GC, (g + 1) * POOL_GC)
            v = pb_s[pl.ds(base + HALO_PB, rc), cols]
            sm = v
            for i in range(1, w):
                sm = sm + pb_s[pl.ds(base + (HALO_PB - i), rc), cols]
            if pos0 + 1 >= w:
                d = sm * (1.0 / w) - v
            else:
                pos = row0 + base + lax.broadcasted_iota(jnp.int32, (rc, POOL_GC), 0)
                d = sm / jnp.minimum(pos + 1, w).astype(jnp.float32) - v
            act_b[pl.ds(base, rc), cols] = d.astype(jnp.bfloat16)
        y = sc_w_ref[0:1, :] * cx_s[pl.ds(base + (HALO_CX - 2), rc), :]
        y = y + sc_w_ref[1:2, :] * cx_s[pl.ds(base + (HALO_CX - 1), rc), :]
        y = y + sc_w_ref[2:3, :] * cx_s[pl.ds(base + HALO_CX, rc), :]
        act_c[pl.ds(base, rc), :] = (cb_s[pl.ds(base, rc), :] * y).astype(jnp.bfloat16)

    new_u = u_s[n_real:n_real + HALO_U, :]
    new_pb = pb_s[n_real:n_real + HALO_PB, :]
    new_cx = cx_s[n_real:n_real + HALO_CX, :]
    u_s[0:HALO_U, :] = new_u
    pb_s[0:HALO_PB, :] = new_pb
    cx_s[0:HALO_CX, :] = new_cx
    su_ref[0] = new_u
    spb_ref[0] = new_pb
    scx_ref[0] = new_cx

    n_wb = D_MODEL // W_BLOCK
    groups_per_wb = W_BLOCK // POOL_OUT
    merged = []
    for nb in range(n_wb):
        cols = slice(nb * W_BLOCK, (nb + 1) * W_BLOCK)
        y_b = jnp.concatenate(
            [_dot(act_b[:, g * POOL_GC:(g + 1) * POOL_GC], pool_w_ref[g])
             for g in range(nb * groups_per_wb, (nb + 1) * groups_per_wb)], axis=-1) * pool_s_ref[:, cols]
        m = gate_s[:, cols] * _dot(act_a[...], w_co_ref[nb])
        m = m + gate_s[:, D_MODEL + nb * W_BLOCK:D_MODEL + (nb + 1) * W_BLOCK] * y_b
        m = m + gate_s[:, 2 * D_MODEL + nb * W_BLOCK:2 * D_MODEL + (nb + 1) * W_BLOCK] * _dot(
            act_c[...], w_so_ref[nb])
        merged.append(m.astype(jnp.bfloat16))
    hout = hin + jnp.concatenate(
        [sum(_dot(merged[kb], w_out_ref[kb, nb]) for kb in range(n_wb)) for nb in range(n_wb)], axis=-1)
    hout_ref[0] = hout

    xn = _rms(hout, nffn_ref[...])
    combo, wa, wb = _route(_dot(xn.astype(jnp.bfloat16), w_r_ref[...]) + b_r_ref[...])

    lane = lax.broadcasted_iota(jnp.int32, (tl, LANES), 1)
    onehot = lane == combo
    prefix = _dot(tri_ref[...], onehot.astype(jnp.bfloat16))
    n_tile = prefix[tl - 1:tl, :]
    cnt_v = alloc[ROW_CNT:ROW_CNT + 1, :]
    cur_v = alloc[ROW_CUR:ROW_CUR + 1, :]
    nf_v = alloc[ROW_NF:ROW_NF + 1, :]
    inv_blk = 1.0 / BLK

    def pick(vec):
        return jnp.sum(jnp.where(onehot, vec, 0.0), axis=-1, keepdims=True)

    rank = pick(prefix + cnt_v) - 1.0
    full_before = jnp.floor((cnt_v + (BLK - 1)) * inv_blk)
    n_new = jnp.floor((cnt_v + n_tile + (BLK - 1)) * inv_blk) - full_before
    new_excl = _dot(jnp.broadcast_to(n_new, (8, LANES)).astype(jnp.bfloat16), sup_ref[...])[0:1, :]
    new_base = nf_v + new_excl
    chunk = jnp.floor(rank * inv_blk)
    blk = jnp.where(chunk < pick(full_before), pick(cur_v), pick(new_base - full_before) + chunk)
    slot = blk * BLK + (rank - chunk * BLK)
    n_new_total = jnp.sum(n_new, axis=-1, keepdims=True)
    alloc[ROW_CNT:ROW_CNT + 1, :] = cnt_v + n_tile
    alloc[ROW_CUR:ROW_CUR + 1, :] = jnp.where(n_new > 0.0, new_base + n_new - 1.0, cur_v)
    alloc[ROW_NF:ROW_NF + 1, :] = nf_v + n_new_total

    sq_sub = lax.broadcasted_iota(jnp.int32, (LANES, LANES), 0)
    sq_lane = lax.broadcasted_iota(jnp.int32, (LANES, LANES), 1)
    new_incl_col = jnp.where(sq_sub == 0, jnp.broadcast_to(new_excl + n_new, (LANES, LANES)), 0.0).T[:, 0:1]
    new_combo = jnp.sum(jnp.where(new_incl_col <= sq_lane.astype(jnp.float32), 1.0, 0.0),
                        axis=0, keepdims=True)
    lane_row = lax.broadcasted_iota(jnp.int32, (1, LANES), 1)
    new_combo = jnp.where(lane_row == LANE_NNEW, n_new_total, new_combo)

    info = jnp.where(lane == 0, slot,
                     jnp.where(lane == LANE_WA - D_MODEL, wa,
                               jnp.where(lane == LANE_WB - D_MODEL, wb, 0.0)))
    scatter_wait()
    if has_moe:
        gather_wait()
    stage[cur_par, :, 0:D_MODEL] = xn
    stage[cur_par, :, D_MODEL:XS_W] = info
    slot_rows = info.T[0:8, :].astype(jnp.int32)
    slot_ref[0] = slot_rows
    meta_v[:, 0:tl] = slot_rows
    meta_v[:, tl:tl + LANES] = jnp.broadcast_to(new_combo, (8, LANES)).astype(jnp.int32)
    meta_copy().start()

    @pl.when(s == n_steps - 1)
    def _():
        meta_copy().wait()
        register_new_blocks()

        def flush(i, c):
            scatter_row(i, meta_s[i], cur_par)
            return c
        lax.fori_loop(0, tl, flush, 0)
        scatter_wait()

        alloc_iv[...] = alloc[...].astype(jnp.int32)
        c0 = pltpu.make_async_copy(alloc_iv.at[ROW_CNT], cnt_s, sem.at[SEM_FLUSH])
        c1 = pltpu.make_async_copy(alloc_iv.at[ROW_CUR], cur_s, sem.at[SEM_FLUSH])
        c0.start()
        c1.start()
        c0.wait()
        c1.wait()

        def close(c, carry):
            n = cnt_s[c]
            bidx = jnp.where(n > 0, cur_s[c], n_blocks)
            blk_nvalid[bidx] = ((n - 1) & (BLK - 1)) + 1
            return carry
        lax.fori_loop(0, LANES, close, 0)
        nused_ref[0] = nf_s[0]


def _const_spec(shape):
    zeros = (0,) * len(shape)
    return pl.BlockSpec(shape, lambda b, j, *_: zeros, pipeline_mode=pl.Buffered(1))


def _mixer(h, moe, halo, lw, *, tl, n_real, pos0):
    bsz, seq, _ = h.shape
    nt = seq // tl
    n_tok = bsz * seq
    n_blocks = _num_blocks(n_tok)
    has_moe = moe is not None
    tile = pl.BlockSpec((1, tl, D_MODEL), lambda b, j, *_: (b, j, 0))
    smem = pl.BlockSpec(memory_space=pltpu.SMEM)
    consts = [halo[0], halo[1], halo[2], lw["norm_mix"], lw["w_in"], lw["conv_w"], lw["conv_b"],
              lw["conv_ln_g"], lw["conv_ln_b"], lw["w_conv_out"], lw["pool_w"], lw["pool_scale"],
              lw["sc_w"], lw["w_sc_out"], lw["w_out"], lw["norm_ffn"], lw["w_route"], lw["b_route"],
              jnp.tril(jnp.ones((tl, tl), jnp.bfloat16)),
              jnp.triu(jnp.ones((LANES, LANES), jnp.bfloat16), 1)]
    in_specs = ([tile] + ([pl.BlockSpec(memory_space=pl.ANY)] if has_moe else [])
                + [_const_spec(c.shape) for c in consts])
    out_shape = (
        jax.ShapeDtypeStruct((bsz, seq, D_MODEL), jnp.float32),
        jax.ShapeDtypeStruct((bsz, HALO_U, D_CONV), jnp.float32),
        jax.ShapeDtypeStruct((bsz, HALO_PB, D_POOL), jnp.float32),
        jax.ShapeDtypeStruct((bsz, HALO_CX, D_SC), jnp.float32),
        jax.ShapeDtypeStruct((n_blocks * BLK + tl, XS_W), jnp.float32),
        jax.ShapeDtypeStruct((bsz * nt, 8, tl), jnp.int32),
        jax.ShapeDtypeStruct((n_blocks + 1,), jnp.int32),
        jax.ShapeDtypeStruct((n_blocks + 1,), jnp.int32),
        jax.ShapeDtypeStruct((1,), jnp.int32),
    )
    out_specs = (
        tile,
        pl.BlockSpec((1, HALO_U, D_CONV), lambda b, j, *_: (b, 0, 0)),
        pl.BlockSpec((1, HALO_PB, D_POOL), lambda b, j, *_: (b, 0, 0)),
        pl.BlockSpec((1, HALO_CX, D_SC), lambda b, j, *_: (b, 0, 0)),
        pl.BlockSpec(memory_space=pl.ANY),
        pl.BlockSpec((1, 8, tl), lambda b, j, *_: (b * nt + j, 0, 0)),
        smem, smem, smem,
    )
    scratch = [
        pltpu.VMEM((HALO_U + tl + SUBLANES, D_CONV), jnp.float32),
        pltpu.VMEM((SUBLANES - 1, HALO_U + tl, D_CONV), jnp.float32),
        pltpu.VMEM((HALO_PB + tl, D_POOL), jnp.float32),
        pltpu.VMEM((HALO_CX + tl, D_SC), jnp.float32),
        pltpu.VMEM((tl, D_SC), jnp.float32),
        pltpu.VMEM((tl, 3 * D_MODEL), jnp.float32),
        pltpu.VMEM((tl, D_CONV), jnp.bfloat16),
        pltpu.VMEM((tl, D_POOL), jnp.bfloat16),
        pltpu.VMEM((tl, D_SC), jnp.bfloat16),
        pltpu.VMEM((2, tl, XS_W), jnp.float32),
        pltpu.VMEM((8, tl + LANES), jnp.int32),
        pltpu.SMEM((tl + LANES,), jnp.int32),
        pltpu.VMEM((8, LANES), jnp.float32),
        pltpu.VMEM((8, LANES), jnp.int32),
        pltpu.SMEM((LANES,), jnp.int32),
        pltpu.SMEM((LANES,), jnp.int32),
        pltpu.SMEM((1,), jnp.int32),
        pltpu.SemaphoreType.DMA((6,)),
    ]
    if has_moe:
        scratch.append(pltpu.VMEM((2, tl, D_MODEL), jnp.float32))
    args = ([moe[0]] if has_moe else []) + [h] + ([moe[1]] if has_moe else []) + consts
    grid_spec = pltpu.PrefetchScalarGridSpec(
        num_scalar_prefetch=1 if has_moe else 0,
        grid=(bsz, nt), in_specs=in_specs, out_specs=out_specs, scratch_shapes=scratch)
    return pl.pallas_call(
        functools.partial(_mixer_kernel, tl=tl, n_real=n_real, pos0=pos0, has_moe=has_moe,
                          n_steps=bsz * nt, n_tok=n_tok, n_blocks=n_blocks),
        grid_spec=grid_spec,
        out_shape=out_shape,
        compiler_params=pltpu.CompilerParams(
            dimension_semantics=("arbitrary", "arbitrary"), vmem_limit_bytes=VMEM_LIMIT),
        name="mixer",
    )(*args)


def _experts_kernel(order_ref, be1_ref, be2_ref, nvalid_ref, nused_ref, xs_ref,
                    w1a_ref, w3a_ref, w2a_ref, w1b_ref, w3b_ref, w2b_ref, out_ref):
    i = pl.program_id(0)

    @pl.when(i < nused_ref[0])
    def _():
        rows = xs_ref[...]
        ok = lax.broadcasted_iota(jnp.int32, (BLK, 1), 0) < nvalid_ref[i]
        x = jnp.where(ok, rows[:, 0:D_MODEL], 0.0).astype(jnp.bfloat16)
        wa = jnp.where(ok, rows[:, LANE_WA:LANE_WA + 1], 0.0)
        wb = jnp.where(ok, rows[:, LANE_WB:LANE_WB + 1], 0.0)

        def expert(w1_ref, w3_ref, w2_ref):
            hid = jax.nn.silu(_dot(x, w1_ref[0])) * _dot(x, w3_ref[0])
            return _dot(hid.astype(jnp.bfloat16), w2_ref[0])

        out_ref[...] = (expert(w1a_ref, w3a_ref, w2a_ref) * wa
                        + expert(w1b_ref, w3b_ref, w2b_ref) * wb)


def _experts(xs, blk_combo, blk_nvalid, nused, lw):
    n_blocks = blk_combo.shape[0] - 1
    ids = jnp.arange(n_blocks, dtype=jnp.int32)
    key = (blk_combo[:n_blocks] << 16) | (ids << 7) | jnp.clip(blk_nvalid[:n_blocks] - 1, 0, BLK - 1)
    key = jnp.sort(key)
    last = lax.dynamic_slice(key, (jnp.maximum(nused[0] - 1, 0),), (1,))
    key = jnp.where(ids < nused[0], key, last)
    combo = key >> 16
    order = (key >> 7) & 511
    nvalid = (key & (BLK - 1)) + 1
    grp = combo // N_PAIRS
    pair = combo % N_PAIRS
    a = jnp.sum(pair[:, None] >= jnp.asarray(_PAIR_OFF)[None, 1:], axis=1).astype(jnp.int32)
    off_a = a * (EPG - 1) - ((a * (a - 1)) >> 1)
    be1 = jnp.minimum(grp * EPG + a, N_EXPERTS - 1).astype(jnp.int32)
    be2 = jnp.minimum(grp * EPG + pair - off_a + a + 1, N_EXPERTS - 1).astype(jnp.int32)

    def wspec(which):
        def index_map(i, order_r, be1_r, be2_r, nv_r, nu_r):
            return ((be1_r, be2_r)[which][i], 0, 0)
        return index_map

    def blk_map(i, order_r, *_):
        return (order_r[i], 0)

    up = (1, D_MODEL, D_EXPERT)
    down = (1, D_EXPERT, D_MODEL)
    grid_spec = pltpu.PrefetchScalarGridSpec(
        num_scalar_prefetch=5,
        grid=(n_blocks,),
        in_specs=[
            pl.BlockSpec((BLK, XS_W), blk_map),
            pl.BlockSpec(up, wspec(0)), pl.BlockSpec(up, wspec(0)), pl.BlockSpec(down, wspec(0)),
            pl.BlockSpec(up, wspec(1)), pl.BlockSpec(up, wspec(1)), pl.BlockSpec(down, wspec(1)),
        ],
        out_specs=pl.BlockSpec((BLK, D_MODEL), blk_map),
    )
    return pl.pallas_call(
        _experts_kernel,
        grid_spec=grid_spec,
        out_shape=jax.ShapeDtypeStruct((n_blocks * BLK, D_MODEL), jnp.float32),
        compiler_params=pltpu.CompilerParams(
            dimension_semantics=("arbitrary",), vmem_limit_bytes=VMEM_LIMIT),
        name="experts",
    )(order.astype(jnp.int32), be1, be2, nvalid.astype(jnp.int32), nused, xs,
      lw["w1"], lw["w3"], lw["w2"], lw["w1"], lw["w3"], lw["w2"])


def _final_kernel(slot_ref, h_ref, moe_hbm, g_ref, out_ref, moe_buf, sem, *, rows, n_steps):
    i = pl.program_id(0)

    n_tok = rows * n_steps

    def row_copy(tile, r, par):
        t = jnp.minimum(tile * rows + r, n_tok - 1)
        return pltpu.make_async_copy(moe_hbm.at[pl.ds(slot_ref[t], 1), :],
                                     moe_buf.at[par, pl.ds(r, 1), :], sem.at[0])

    def wait_tile():
        pltpu.make_async_copy(moe_hbm.at[pl.ds(0, rows), :], moe_buf.at[0], sem.at[0]).wait()

    @pl.when(i == 0)
    def _():
        def body(r, c):
            row_copy(0, r, 0).start()
            return c
        lax.fori_loop(0, rows, body, 0)
        wait_tile()

    par = i % 2
    for r in range(rows):
        row_copy(i + 1, r, 1 - par).start()
    out_ref[...] = _rms(h_ref[...] + moe_buf[par], g_ref[...])
    wait_tile()


def _final_norm(h, slot, moe_rows, g, rows):
    n = h.shape[0]
    tile = pl.BlockSpec((rows, D_MODEL), lambda i, *_: (i, 0))
    grid_spec = pltpu.PrefetchScalarGridSpec(
        num_scalar_prefetch=1,
        grid=(n // rows,),
        in_specs=[tile, pl.BlockSpec(memory_space=pl.ANY),
                  pl.BlockSpec((1, D_MODEL), lambda i, *_: (0, 0))],
        out_specs=tile,
        scratch_shapes=[pltpu.VMEM((2, rows, D_MODEL), jnp.float32), pltpu.SemaphoreType.DMA((2,))],
    )
    return pl.pallas_call(
        functools.partial(_final_kernel, rows=rows, n_steps=n // rows),
        grid_spec=grid_spec,
        out_shape=jax.ShapeDtypeStruct((n, D_MODEL), jnp.float32),
        compiler_params=pltpu.CompilerParams(
            dimension_semantics=("arbitrary",), vmem_limit_bytes=VMEM_LIMIT),
        name="final_norm",
    )(slot, h, moe_rows, g)


def _layer_weights(p, l):
    bf = jnp.bfloat16
    w_route = jnp.zeros((D_MODEL, LANES), jnp.float32)
    w_route = w_route.at[:, :N_GROUPS].set(p["w_group"][l])
    w_route = w_route.at[:, ROUTE_E0:ROUTE_E0 + N_EXPERTS].set(p["w_router"][l])
    b_route = jnp.zeros((1, LANES), jnp.float32)
    b_route = b_route.at[0, :N_GROUPS].set(p["b_group"][l])
    b_route = b_route.at[0, ROUTE_E0:ROUTE_E0 + N_EXPERTS].set(p["b_router"][l])

    def col_blocks(w):
        k, n = w.shape
        return w.reshape(k, n // W_BLOCK, W_BLOCK).transpose(1, 0, 2).astype(bf)

    col = jnp.arange(p["w_in"].shape[-1])
    halved = ((col < C_PB) | (col >= C_GA))
    w_in = p["w_in"][l] * jnp.where(halved, 0.5, 1.0)[None, :]
    w_out = (0.5 * p["w_out"][l]).reshape(D_MODEL // W_BLOCK, W_BLOCK, D_MODEL // W_BLOCK, W_BLOCK)
    return {
        "norm_mix": p["norm_mix"][l][None, :],
        "w_in": col_blocks(w_in),
        "conv_w": jnp.broadcast_to(p["conv_w"][l][:, None, :], (CONV_K, SUBLANES, D_CONV)),
        "conv_b": p["conv_b"][l][None, :],
        "conv_ln_g": p["conv_ln_g"][l][None, :],
        "conv_ln_b": p["conv_ln_b"][l][None, :],
        "w_conv_out": col_blocks(p["w_conv_out"][l]),
        "pool_w": p["pool_w"][l].astype(bf),
        "pool_scale": p["pool_scale"][l][None, :],
        "sc_w": p["sc_w"][l],
        "w_sc_out": col_blocks(p["w_sc_out"][l]),
        "w_out": w_out.transpose(0, 2, 1, 3).astype(bf),
        "norm_ffn": p["norm_ffn"][l][None, :],
        "w_route": w_route.astype(bf),
        "b_route": b_route,
        "w1": p["w1"][l].astype(bf),
        "w3": p["w3"][l].astype(bf),
        "w2": p["w2"][l].astype(bf),
    }


def _layer(h, moe, halo, lw, *, tl, n_real, pos0):
    h, su, spb, scx, xs, slot, blk_combo, blk_nvalid, nused = _mixer(
        h, moe, halo, lw, tl=tl, n_real=n_real, pos0=pos0)
    out = _experts(xs, blk_combo, blk_nvalid, nused, lw)
    return h, (slot[:, 0, :].reshape(-1), out), (su[0], spb[0], scx[0])


def kernel(x, meta, norm_mix, w_in, conv_w, conv_b, conv_ln_g, conv_ln_b, w_conv_out, pool_w,
           pool_scale, sc_w, w_sc_out, w_out, norm_ffn, w_group, b_group, w_router, b_router,
           w1, w3, w2, final_norm):
    p = dict(norm_mix=norm_mix, w_in=w_in, conv_w=conv_w, conv_b=conv_b, conv_ln_g=conv_ln_g,
             conv_ln_b=conv_ln_b, w_conv_out=w_conv_out, pool_w=pool_w, pool_scale=pool_scale,
             sc_w=sc_w, w_sc_out=w_sc_out, w_out=w_out, norm_ffn=norm_ffn, w_group=w_group,
             b_group=b_group, w_router=w_router, b_router=b_router, w1=w1, w3=w3, w2=w2)
    bsz, seq, _ = x.shape
    n_tok = bsz * seq
    n_layers = norm_mix.shape[0]
    lws = [_layer_weights(p, l) for l in range(n_layers)]
    zero_halo = (jnp.zeros((HALO_U, D_CONV), jnp.float32),
                 jnp.zeros((HALO_PB, D_POOL), jnp.float32),
                 jnp.zeros((HALO_CX, D_SC), jnp.float32))

    halos = []
    hm = jnp.zeros((1, META_TILE, D_MODEL), jnp.float32).at[0, :N_META].set(meta)
    moe_m = None
    for l, lw in enumerate(lws):
        if l + 1 < n_layers:
            hm, moe_m, halo = _layer(hm, moe_m, zero_halo, lw, tl=META_TILE, n_real=N_META, pos0=0)
        else:
            halo = _mixer(hm, moe_m, zero_halo, lw, tl=META_TILE, n_real=N_META, pos0=0)[1:4]
            halo = tuple(a[0] for a in halo)
        halos.append(halo)

    h = x
    moe = None
    for l, lw in enumerate(lws):
        h, moe, _ = _layer(h, moe, halos[l], lw, tl=MAIN_TILE, n_real=MAIN_TILE, pos0=N_META)
    out = _final_norm(h.reshape(n_tok, D_MODEL), moe[0], moe[1], final_norm[None, :], MAIN_TILE)
    return out.reshape(bsz, seq, D_MODEL)
```

```python
import functools

import numpy as np
import jax
import jax.numpy as jnp
from jax import lax
from jax.experimental import pallas as pl
from jax.experimental.pallas import tpu as pltpu

D_MODEL = 1024
N_META = 16
D_CONV = 512
CONV_K = 31
D_POOL = 512
POOL_WINDOWS = (2, 4, 8, 16)
POOL_GC = 128
POOL_OUT = 256
D_SC = 512
SC_K = 3
N_GROUPS = 4
EPG = 8
N_EXPERTS = 32
D_EXPERT = 256
EPS = 1e-6

C_ALIN, C_AGATE, C_PB, C_CB, C_CC, C_CX, C_GA, C_GB, C_GC = (
    0, 512, 1024, 1536, 2048, 2560, 3072, 4096, 5120)

LANES = 128
SUBLANES = 8
N_PAIRS = EPG * (EPG - 1) // 2
N_COMBO = N_GROUPS * N_PAIRS
ROUTE_E0 = 4
NO_COMBO = LANES - 1

HALO_U = 32
HALO_PB = 16
HALO_CX = 8
W_BLOCK = 512
GATE_BLOCK = W_BLOCK
ROW_CHUNK = 32

BLK = 256
XS_W = D_MODEL + LANES
LANE_WA, LANE_WB = D_MODEL + 1, D_MODEL + 2
MAIN_TILE = 512
META_TILE = 128
VMEM_LIMIT = 56 * 1024 * 1024

_PAIR_OFF = np.cumsum([0] + [EPG - 1 - a for a in range(EPG - 1)])[:-1].astype(np.int32)


def _sigmoid(x):
    return 0.5 * jnp.tanh(0.5 * x) + 0.5


def _rms(x, g):
    return x * lax.rsqrt(jnp.mean(x * x, axis=-1, keepdims=True) + EPS) * g


def _dot(a, b):
    return jnp.dot(a, b, preferred_element_type=jnp.float32)


def _num_blocks(n_tok):
    return n_tok // BLK + min(n_tok, N_COMBO)


def _route(logits):
    n = logits.shape[0]
    lane = lax.broadcasted_iota(jnp.int32, (n, LANES), 1)
    neg = jnp.float32(-jnp.inf)

    def first_max(vals):
        m = jnp.max(vals, axis=-1, keepdims=True)
        idx = jnp.min(jnp.where(vals == m, lane, LANES), axis=-1, keepdims=True)
        return m, idx

    is_g = lane < N_GROUPS
    lg = jnp.where(is_g, logits, neg)
    mg, g_sel = first_max(lg)
    p_sel = 1.0 / jnp.sum(jnp.where(is_g, jnp.exp(lg - mg), 0.0), axis=-1, keepdims=True)

    lo = ROUTE_E0 + EPG * g_sel
    le = jnp.where((lane >= lo) & (lane < lo + EPG), logits, neg)
    v1, i1 = first_max(le)
    v2, i2 = first_max(jnp.where(lane == i1, neg, le))
    t = jnp.exp(v2 - v1)
    w1 = p_sel / (1.0 + t)
    w2 = p_sel * t / (1.0 + t)

    e1 = i1 - lo
    e2 = i2 - lo
    a = jnp.minimum(e1, e2)
    b = jnp.maximum(e1, e2)
    wa = jnp.where(e1 < e2, w1, w2)
    wb = jnp.where(e1 < e2, w2, w1)
    pair = a * (EPG - 1) - ((a * (a - 1)) >> 1) + (b - a - 1)
    return g_sel * N_PAIRS + pair, wa, wb


def _mixer_kernel(*refs, tl, n_real, pos0, has_moe, n_steps, n_tok, n_blocks):
    it = iter(refs)
    slot_prev = next(it) if has_moe else None
    h_ref = next(it)
    moe_hbm = next(it) if has_moe else None
    (hu_ref, hpb_ref, hcx_ref, nmix_ref, w_in_ref, conv_w_ref, conv_b_ref, ln_g_ref, ln_b_ref,
     w_co_ref, pool_w_ref, pool_s_ref, sc_w_ref, w_so_ref, w_out_ref, nffn_ref, w_r_ref, b_r_ref,
     tri_ref, sup_ref,
     hout_ref, su_ref, spb_ref, scx_ref, xs_hbm, slot_ref, blk_combo, blk_nvalid, nused_ref,
     u_s, u_sh, pb_s, cx_s, cb_s, gate_s, act_a, act_b, act_c, stage, meta_v, meta_s, alloc, alloc_iv,
     cnt_s, cur_s, nf_s, sem) = [next(it) for _ in range(47)]
    moe_buf = next(it) if has_moe else None

    j = pl.program_id(1)
    s = pl.program_id(0) * pl.num_programs(1) + j
    dump_row0 = n_blocks * BLK
    SEM_SCATTER, SEM_META, SEM_FLUSH, SEM_GATHER = 0, 1, 2, 3
    ROW_CNT, ROW_CUR, ROW_NF = 0, 1, 2
    LANE_NNEW = LANES - 1

    def meta_copy():
        return pltpu.make_async_copy(meta_v.at[0], meta_s, sem.at[SEM_META])

    def gather_copy(tile, i, par):
        t = jnp.minimum(tile * tl + i, n_tok - 1)
        return pltpu.make_async_copy(
            moe_hbm.at[pl.ds(slot_prev[t], 1), :], moe_buf.at[par, pl.ds(i, 1), :], sem.at[SEM_GATHER])

    def gather_wait():
        pltpu.make_async_copy(moe_hbm.at[pl.ds(0, tl), :], moe_buf.at[0], sem.at[SEM_GATHER]).wait()

    def scatter_row(i, slot, par):
        pltpu.make_async_copy(stage.at[par, pl.ds(i, 1), :], xs_hbm.at[pl.ds(slot, 1), :],
                              sem.at[SEM_SCATTER]).start()

    def scatter_wait():
        pltpu.make_async_copy(stage.at[0], xs_hbm.at[pl.ds(0, tl), :], sem.at[SEM_SCATTER]).wait()

    def register_new_blocks():
        first = nf_s[0]
        n_new = meta_s[tl + LANE_NNEW]

        def body(o, c):
            blk_combo[first + o] = meta_s[tl + o]
            blk_nvalid[first + o] = BLK
            return c
        lax.fori_loop(0, n_new, body, 0)
        nf_s[0] = first + n_new

    @pl.when(s == 0)
    def _():
        def unused(i, c):
            blk_combo[i] = NO_COMBO
            blk_nvalid[i] = 0
            return c
        lax.fori_loop(0, n_blocks + 1, unused, 0)
        nf_s[0] = 0
        alloc[...] = jnp.zeros_like(alloc)
        meta_v[...] = jnp.zeros_like(meta_v)
        meta_copy().start()
        stage[...] = jnp.zeros_like(stage)
        u_s[HALO_U + tl:HALO_U + tl + SUBLANES, :] = jnp.zeros((SUBLANES, D_CONV), jnp.float32)
        if has_moe:
            def first(i, c):
                gather_copy(0, i, 0).start()
                return c
            lax.fori_loop(0, tl, first, 0)
            gather_wait()

    @pl.when(j == 0)
    def _():
        u_s[0:HALO_U, :] = hu_ref[...]
        pb_s[0:HALO_PB, :] = hpb_ref[...]
        cx_s[0:HALO_CX, :] = hcx_ref[...]

    meta_copy().wait()
    register_new_blocks()

    prev_par = (s + 1) % 2
    cur_par = s % 2
    for i in range(tl):
        scatter_row(i, jnp.where(s > 0, meta_s[i], dump_row0 + i), prev_par)

    if has_moe:
        hin = h_ref[0] + moe_buf[cur_par]
        for i in range(tl):
            gather_copy(s + 1, i, prev_par).start()
    else:
        hin = h_ref[0]
    xb = _rms(hin, nmix_ref[...]).astype(jnp.bfloat16)

    def proj(c0):
        return _dot(xb, w_in_ref[c0 // W_BLOCK])

    u_s[HALO_U:HALO_U + tl, :] = proj(C_ALIN) * (jnp.tanh(proj(C_AGATE)) + 1.0)
    pb_s[HALO_PB:HALO_PB + tl, :] = proj(C_PB)
    cx_s[HALO_CX:HALO_CX + tl, :] = proj(C_CC) * proj(C_CX)
    cb_s[...] = proj(C_CB)
    for lo in range(1, SUBLANES):
        u_sh[lo - 1] = u_s[lo:lo + HALO_U + tl, :]

    rc = min(ROW_CHUNK, tl)
    row0 = pos0 + j * tl

    n_chunks = tl // rc
    n_gate_blocks = 3 * D_MODEL // GATE_BLOCK

    for ci, base in enumerate(range(0, tl, rc)):
        for gb in range(ci * n_gate_blocks // n_chunks, (ci + 1) * n_gate_blocks // n_chunks):
            c0 = gb * GATE_BLOCK
            gate_s[:, c0:c0 + GATE_BLOCK] = jnp.tanh(proj(C_GA + c0)) + 1.0
        acc = jnp.broadcast_to(conv_b_ref[...], (rc, D_CONV))
        for k in range(CONV_K):
            hi, lo = divmod(HALO_U - (CONV_K - 1) + k, SUBLANES)
            src = u_s if lo == 0 else u_sh.at[lo - 1]
            w_k = jnp.concatenate([conv_w_ref[k]] * (rc // SUBLANES), axis=0)
            acc = acc + w_k * src[pl.ds(base + SUBLANES * hi, rc), :]
        mu = jnp.mean(acc, axis=-1, keepdims=True)
        dev = acc - mu
        var = jnp.mean(dev * dev, axis=-1, keepdims=True)
        yn = dev * lax.rsqrt(var + EPS) * ln_g_ref[...] + ln_b_ref[...]
        act_a[pl.ds(base, rc), :] = (yn * _sigmoid(yn)).astype(jnp.bfloat16)
        for g, w in enumerate(POOL_WINDOWS):
            cols = slice(g * POOL_GC, (g + 1) * POOL_GC)
            v = pb_s[pl.ds(base + HALO_PB, rc), cols]
            sm = v
            for i in range(1, w):
                sm = sm + pb_s[pl.ds(base + (HALO_PB - i), rc), cols]
            if pos0 + 1 >= w:
                d = sm * (1.0 / w) - v
            else:
                pos = row0 + base + lax.broadcasted_iota(jnp.int32, (rc, POOL_GC), 0)
                d = sm / jnp.minimum(pos + 1, w).astype(jnp.float32) - v
            act_b[pl.ds(base, rc), cols] = d.astype(jnp.bfloat16)
        y = sc_w_ref[0:1, :] * cx_s[pl.ds(base + (HALO_CX - 2), rc), :]
        y = y + sc_w_ref[1:2, :] * cx_s[pl.ds(base + (HALO_CX - 1), rc), :]
        y = y + sc_w_ref[2:3, :] * cx_s[pl.ds(base + HALO_CX, rc), :]
        act_c[pl.ds(base, rc), :] = (cb_s[pl.ds(base, rc), :] * y).astype(jnp.bfloat16)

    new_u = u_s[n_real:n_real + HALO_U, :]
    new_pb = pb_s[n_real:n_real + HALO_PB, :]
    new_cx = cx_s[n_real:n_real + HALO_CX, :]
    u_s[0:HALO_U, :] = new_u
    pb_s[0:HALO_PB, :] = new_pb
    cx_s[0:HALO_CX, :] = new_cx
    su_ref[0] = new_u
    spb_ref[0] = new_pb
    scx_ref[0] = new_cx

    n_wb = D_MODEL // W_BLOCK
    groups_per_wb = W_BLOCK // POOL_OUT
    merged = []
    for nb in range(n_wb):
        cols = slice(nb * W_BLOCK, (nb + 1) * W_BLOCK)
        y_b = jnp.concatenate(
            [_dot(act_b[:, g * POOL_GC:(g + 1) * POOL_GC], pool_w_ref[g])
             for g in range(nb * groups_per_wb, (nb + 1) * groups_per_wb)], axis=-1) * pool_s_ref[:, cols]
        m = gate_s[:, cols] * _dot(act_a[...], w_co_ref[nb])
        m = m + gate_s[:, D_MODEL + nb * W_BLOCK:D_MODEL + (nb + 1) * W_BLOCK] * y_b
        m = m + gate_s[:, 2 * D_MODEL + nb * W_BLOCK:2 * D_MODEL + (nb + 1) * W_BLOCK] * _dot(
            act_c[...], w_so_ref[nb])
        merged.append(m.astype(jnp.bfloat16))
    hout = hin + jnp.concatenate(
        [sum(_dot(merged[kb], w_out_ref[kb, nb]) for kb in range(n_wb)) for nb in range(n_wb)], axis=-1)
    hout_ref[0] = hout

    xn = _rms(hout, nffn_ref[...])
    combo, wa, wb = _route(_dot(xn.astype(jnp.bfloat16), w_r_ref[...]) + b_r_ref[...])

    lane = lax.broadcasted_iota(jnp.int32, (tl, LANES), 1)
    onehot = lane == combo
    prefix = _dot(tri_ref[...], onehot.astype(jnp.bfloat16))
    n_tile = prefix[tl - 1:tl, :]
    cnt_v = alloc[ROW_CNT:ROW_CNT + 1, :]
    cur_v = alloc[ROW_CUR:ROW_CUR + 1, :]
    nf_v = alloc[ROW_NF:ROW_NF + 1, :]
    inv_blk = 1.0 / BLK

    def pick(vec):
        return jnp.sum(jnp.where(onehot, vec, 0.0), axis=-1, keepdims=True)

    rank = pick(prefix + cnt_v) - 1.0
    full_before = jnp.floor((cnt_v + (BLK - 1)) * inv_blk)
    n_new = jnp.floor((cnt_v + n_tile + (BLK - 1)) * inv_blk) - full_before
    new_excl = _dot(jnp.broadcast_to(n_new, (8, LANES)).astype(jnp.bfloat16), sup_ref[...])[0:1, :]
    new_base = nf_v + new_excl
    chunk = jnp.floor(rank * inv_blk)
    blk = jnp.where(chunk < pick(full_before), pick(cur_v), pick(new_base - full_before) + chunk)
    slot = blk * BLK + (rank - chunk * BLK)
    n_new_total = jnp.sum(n_new, axis=-1, keepdims=True)
    alloc[ROW_CNT:ROW_CNT + 1, :] = cnt_v + n_tile
    alloc[ROW_CUR:ROW_CUR + 1, :] = jnp.where(n_new > 0.0, new_base + n_new - 1.0, cur_v)
    alloc[ROW_NF:ROW_NF + 1, :] = nf_v + n_new_total

    sq_sub = lax.broadcasted_iota(jnp.int32, (LANES, LANES), 0)
    sq_lane = lax.broadcasted_iota(jnp.int32, (LANES, LANES), 1)
    new_incl_col = jnp.where(sq_sub == 0, jnp.broadcast_to(new_excl + n_new, (LANES, LANES)), 0.0).T[:, 0:1]
    new_combo = jnp.sum(jnp.where(new_incl_col <= sq_lane.astype(jnp.float32), 1.0, 0.0),
                        axis=0, keepdims=True)
    lane_row = lax.broadcasted_iota(jnp.int32, (1, LANES), 1)
    new_combo = jnp.where(lane_row == LANE_NNEW, n_new_total, new_combo)

    info = jnp.where(lane == 0, slot,
                     jnp.where(lane == LANE_WA - D_MODEL, wa,
                               jnp.where(lane == LANE_WB - D_MODEL, wb, 0.0)))
    scatter_wait()
    if has_moe:
        gather_wait()
    stage[cur_par, :, 0:D_MODEL] = xn
    stage[cur_par, :, D_MODEL:XS_W] = info
    slot_rows = info.T[0:8, :].astype(jnp.int32)
    slot_ref[0] = slot_rows
    meta_v[:, 0:tl] = slot_rows
    meta_v[:, tl:tl + LANES] = jnp.broadcast_to(new_combo, (8, LANES)).astype(jnp.int32)
    meta_copy().start()

    @pl.when(s == n_steps - 1)
    def _():
        meta_copy().wait()
        register_new_blocks()

        def flush(i, c):
            scatter_row(i, meta_s[i], cur_par)
            return c
        lax.fori_loop(0, tl, flush, 0)
        scatter_wait()

        alloc_iv[...] = alloc[...].astype(jnp.int32)
        c0 = pltpu.make_async_copy(alloc_iv.at[ROW_CNT], cnt_s, sem.at[SEM_FLUSH])
        c1 = pltpu.make_async_copy(alloc_iv.at[ROW_CUR], cur_s, sem.at[SEM_FLUSH])
        c0.start()
        c1.start()
        c0.wait()
        c1.wait()

        def close(c, carry):
            n = cnt_s[c]
            bidx = jnp.where(n > 0, cur_s[c], n_blocks)
            blk_nvalid[bidx] = ((n - 1) & (BLK - 1)) + 1
            return carry
        lax.fori_loop(0, LANES, close, 0)
        nused_ref[0] = nf_s[0]


def _const_spec(shape):
    zeros = (0,) * len(shape)
    return pl.BlockSpec(shape, lambda b, j, *_: zeros, pipeline_mode=pl.Buffered(1))


def _mixer(h, moe, halo, lw, *, tl, n_real, pos0):
    bsz, seq, _ = h.shape
    nt = seq // tl
    n_tok = bsz * seq
    n_blocks = _num_blocks(n_tok)
    has_moe = moe is not None
    tile = pl.BlockSpec((1, tl, D_MODEL), lambda b, j, *_: (b, j, 0))
    smem = pl.BlockSpec(memory_space=pltpu.SMEM)
    consts = [halo[0], halo[1], halo[2], lw["norm_mix"], lw["w_in"], lw["conv_w"], lw["conv_b"],
              lw["conv_ln_g"], lw["conv_ln_b"], lw["w_conv_out"], lw["pool_w"], lw["pool_scale"],
              lw["sc_w"], lw["w_sc_out"], lw["w_out"], lw["norm_ffn"], lw["w_route"], lw["b_route"],
              jnp.tril(jnp.ones((tl, tl), jnp.bfloat16)),
              jnp.triu(jnp.ones((LANES, LANES), jnp.bfloat16), 1)]
    in_specs = ([tile] + ([pl.BlockSpec(memory_space=pl.ANY)] if has_moe else [])
                + [_const_spec(c.shape) for c in consts])
    out_shape = (
        jax.ShapeDtypeStruct((bsz, seq, D_MODEL), jnp.float32),
        jax.ShapeDtypeStruct((bsz, HALO_U, D_CONV), jnp.float32),
        jax.ShapeDtypeStruct((bsz, HALO_PB, D_POOL), jnp.float32),
        jax.ShapeDtypeStruct((bsz, HALO_CX, D_SC), jnp.float32),
        jax.ShapeDtypeStruct((n_blocks * BLK + tl, XS_W), jnp.float32),
        jax.ShapeDtypeStruct((bsz * nt, 8, tl), jnp.int32),
        jax.ShapeDtypeStruct((n_blocks + 1,), jnp.int32),
        jax.ShapeDtypeStruct((n_blocks + 1,), jnp.int32),
        jax.ShapeDtypeStruct((1,), jnp.int32),
    )
    out_specs = (
        tile,
        pl.BlockSpec((1, HALO_U, D_CONV), lambda b, j, *_: (b, 0, 0)),
        pl.BlockSpec((1, HALO_PB, D_POOL), lambda b, j, *_: (b, 0, 0)),
        pl.BlockSpec((1, HALO_CX, D_SC), lambda b, j, *_: (b, 0, 0)),
        pl.BlockSpec(memory_space=pl.ANY),
        pl.BlockSpec((1, 8, tl), lambda b, j, *_: (b * nt + j, 0, 0)),
        smem, smem, smem,
    )
    scratch = [
        pltpu.VMEM((HALO_U + tl + SUBLANES, D_CONV), jnp.float32),
        pltpu.VMEM((SUBLANES - 1, HALO_U + tl, D_CONV), jnp.float32),
        pltpu.VMEM((HALO_PB + tl, D_POOL), jnp.float32),
        pltpu.VMEM((HALO_CX + tl, D_SC), jnp.float32),
        pltpu.VMEM((tl, D_SC), jnp.float32),
        pltpu.VMEM((tl, 3 * D_MODEL), jnp.float32),
        pltpu.VMEM((tl, D_CONV), jnp.bfloat16),
        pltpu.VMEM((tl, D_POOL), jnp.bfloat16),
        pltpu.VMEM((tl, D_SC), jnp.bfloat16),
        pltpu.VMEM((2, tl, XS_W), jnp.float32),
        pltpu.VMEM((8, tl + LANES), jnp.int32),
        pltpu.SMEM((tl + LANES,), jnp.int32),
        pltpu.VMEM((8, LANES), jnp.float32),
        pltpu.VMEM((8, LANES), jnp.int32),
        pltpu.SMEM((LANES,), jnp.int32),
        pltpu.SMEM((LANES,), jnp.int32),
        pltpu.SMEM((1,), jnp.int32),
        pltpu.SemaphoreType.DMA((6,)),
    ]
    if has_moe:
        scratch.append(pltpu.VMEM((2, tl, D_MODEL), jnp.float32))
    args = ([moe[0]] if has_moe else []) + [h] + ([moe[1]] if has_moe else []) + consts
    grid_spec = pltpu.PrefetchScalarGridSpec(
        num_scalar_prefetch=1 if has_moe else 0,
        grid=(bsz, nt), in_specs=in_specs, out_specs=out_specs, scratch_shapes=scratch)
    return pl.pallas_call(
        functools.partial(_mixer_kernel, tl=tl, n_real=n_real, pos0=pos0, has_moe=has_moe,
                          n_steps=bsz * nt, n_tok=n_tok, n_blocks=n_blocks),
        grid_spec=grid_spec,
        out_shape=out_shape,
        compiler_params=pltpu.CompilerParams(
            dimension_semantics=("arbitrary", "arbitrary"), vmem_limit_bytes=VMEM_LIMIT),
        name="mixer",
    )(*args)


def _experts_kernel(order_ref, be1_ref, be2_ref, nvalid_ref, nused_ref, xs_ref,
                    w1a_ref, w3a_ref, w2a_ref, w1b_ref, w3b_ref, w2b_ref, out_ref):
    i = pl.program_id(0)

    @pl.when(i < nused_ref[0])
    def _():
        rows = xs_ref[...]
        ok = lax.broadcasted_iota(jnp.int32, (BLK, 1), 0) < nvalid_ref[i]
        x = jnp.where(ok, rows[:, 0:D_MODEL], 0.0).astype(jnp.bfloat16)
        wa = jnp.where(ok, rows[:, LANE_WA:LANE_WA + 1], 0.0)
        wb = jnp.where(ok, rows[:, LANE_WB:LANE_WB + 1], 0.0)

        def expert(w1_ref, w3_ref, w2_ref):
            hid = jax.nn.silu(_dot(x, w1_ref[0])) * _dot(x, w3_ref[0])
            return _dot(hid.astype(jnp.bfloat16), w2_ref[0])

        out_ref[...] = (expert(w1a_ref, w3a_ref, w2a_ref) * wa
                        + expert(w1b_ref, w3b_ref, w2b_ref) * wb)


def _experts(xs, blk_combo, blk_nvalid, nused, lw):
    n_blocks = blk_combo.shape[0] - 1
    ids = jnp.arange(n_blocks, dtype=jnp.int32)
    fill_bits = BLK.bit_length() - 1
    id_bits = max(n_blocks - 1, 1).bit_length()
    key = ((blk_combo[:n_blocks] << (id_bits + fill_bits)) | (ids << fill_bits)
           | jnp.clip(blk_nvalid[:n_blocks] - 1, 0, BLK - 1))
    key = jnp.sort(key)
    last = lax.dynamic_slice(key, (jnp.maximum(nused[0] - 1, 0),), (1,))
    key = jnp.where(ids < nused[0], key, last)
    combo = key >> (id_bits + fill_bits)
    order = (key >> fill_bits) & ((1 << id_bits) - 1)
    nvalid = (key & (BLK - 1)) + 1
    grp = combo // N_PAIRS
    pair = combo % N_PAIRS
    a = jnp.sum(pair[:, None] >= jnp.asarray(_PAIR_OFF)[None, 1:], axis=1).astype(jnp.int32)
    off_a = a * (EPG - 1) - ((a * (a - 1)) >> 1)
    be1 = jnp.minimum(grp * EPG + a, N_EXPERTS - 1).astype(jnp.int32)
    be2 = jnp.minimum(grp * EPG + pair - off_a + a + 1, N_EXPERTS - 1).astype(jnp.int32)

    def wspec(which):
        def index_map(i, order_r, be1_r, be2_r, nv_r, nu_r):
            return ((be1_r, be2_r)[which][i], 0, 0)
        return index_map

    def blk_map(i, order_r, *_):
        return (order_r[i], 0)

    up = (1, D_MODEL, D_EXPERT)
    down = (1, D_EXPERT, D_MODEL)
    grid_spec = pltpu.PrefetchScalarGridSpec(
        num_scalar_prefetch=5,
        grid=(n_blocks,),
        in_specs=[
            pl.BlockSpec((BLK, XS_W), blk_map),
            pl.BlockSpec(up, wspec(0)), pl.BlockSpec(up, wspec(0)), pl.BlockSpec(down, wspec(0)),
            pl.BlockSpec(up, wspec(1)), pl.BlockSpec(up, wspec(1)), pl.BlockSpec(down, wspec(1)),
        ],
        out_specs=pl.BlockSpec((BLK, D_MODEL), blk_map),
    )
    return pl.pallas_call(
        _experts_kernel,
        grid_spec=grid_spec,
        out_shape=jax.ShapeDtypeStruct((n_blocks * BLK, D_MODEL), jnp.float32),
        compiler_params=pltpu.CompilerParams(
            dimension_semantics=("arbitrary",), vmem_limit_bytes=VMEM_LIMIT),
        name="experts",
    )(order.astype(jnp.int32), be1, be2, nvalid.astype(jnp.int32), nused, xs,
      lw["w1"], lw["w3"], lw["w2"], lw["w1"], lw["w3"], lw["w2"])


def _final_kernel(slot_ref, h_ref, moe_hbm, g_ref, out_ref, moe_buf, sem, *, rows, n_steps):
    i = pl.program_id(0)

    n_tok = rows * n_steps

    def row_copy(tile, r, par):
        t = jnp.minimum(tile * rows + r, n_tok - 1)
        return pltpu.make_async_copy(moe_hbm.at[pl.ds(slot_ref[t], 1), :],
                                     moe_buf.at[par, pl.ds(r, 1), :], sem.at[0])

    def wait_tile():
        pltpu.make_async_copy(moe_hbm.at[pl.ds(0, rows), :], moe_buf.at[0], sem.at[0]).wait()

    @pl.when(i == 0)
    def _():
        def body(r, c):
            row_copy(0, r, 0).start()
            return c
        lax.fori_loop(0, rows, body, 0)
        wait_tile()

    par = i % 2
    for r in range(rows):
        row_copy(i + 1, r, 1 - par).start()
    out_ref[...] = _rms(h_ref[...] + moe_buf[par], g_ref[...])
    wait_tile()


def _final_norm(h, slot, moe_rows, g, rows):
    n = h.shape[0]
    tile = pl.BlockSpec((rows, D_MODEL), lambda i, *_: (i, 0))
    grid_spec = pltpu.PrefetchScalarGridSpec(
        num_scalar_prefetch=1,
        grid=(n // rows,),
        in_specs=[tile, pl.BlockSpec(memory_space=pl.ANY),
                  pl.BlockSpec((1, D_MODEL), lambda i, *_: (0, 0))],
        out_specs=tile,
        scratch_shapes=[pltpu.VMEM((2, rows, D_MODEL), jnp.float32), pltpu.SemaphoreType.DMA((2,))],
    )
    return pl.pallas_call(
        functools.partial(_final_kernel, rows=rows, n_steps=n // rows),
        grid_spec=grid_spec,
        out_shape=jax.ShapeDtypeStruct((n, D_MODEL), jnp.float32),
        compiler_params=pltpu.CompilerParams(
            dimension_semantics=("arbitrary",), vmem_limit_bytes=VMEM_LIMIT),
        name="final_norm",
    )(slot, h, moe_rows, g)


def _layer_weights(p, l):
    bf = jnp.bfloat16
    w_route = jnp.zeros((D_MODEL, LANES), jnp.float32)
    w_route = w_route.at[:, :N_GROUPS].set(p["w_group"][l])
    w_route = w_route.at[:, ROUTE_E0:ROUTE_E0 + N_EXPERTS].set(p["w_router"][l])
    b_route = jnp.zeros((1, LANES), jnp.float32)
    b_route = b_route.at[0, :N_GROUPS].set(p["b_group"][l])
    b_route = b_route.at[0, ROUTE_E0:ROUTE_E0 + N_EXPERTS].set(p["b_router"][l])

    def col_blocks(w):
        k, n = w.shape
        return w.reshape(k, n // W_BLOCK, W_BLOCK).transpose(1, 0, 2).astype(bf)

    col = jnp.arange(p["w_in"].shape[-1])
    halved = ((col < C_PB) | (col >= C_GA))
    w_in = p["w_in"][l] * jnp.where(halved, 0.5, 1.0)[None, :]
    w_out = (0.5 * p["w_out"][l]).reshape(D_MODEL // W_BLOCK, W_BLOCK, D_MODEL // W_BLOCK, W_BLOCK)
    return {
        "norm_mix": p["norm_mix"][l][None, :],
        "w_in": col_blocks(w_in),
        "conv_w": jnp.broadcast_to(p["conv_w"][l][:, None, :], (CONV_K, SUBLANES, D_CONV)),
        "conv_b": p["conv_b"][l][None, :],
        "conv_ln_g": p["conv_ln_g"][l][None, :],
        "conv_ln_b": p["conv_ln_b"][l][None, :],
        "w_conv_out": col_blocks(p["w_conv_out"][l]),
        "pool_w": p["pool_w"][l].astype(bf),
        "pool_scale": p["pool_scale"][l][None, :],
        "sc_w": p["sc_w"][l],
        "w_sc_out": col_blocks(p["w_sc_out"][l]),
        "w_out": w_out.transpose(0, 2, 1, 3).astype(bf),
        "norm_ffn": p["norm_ffn"][l][None, :],
        "w_route": w_route.astype(bf),
        "b_route": b_route,
        "w1": p["w1"][l].astype(bf),
        "w3": p["w3"][l].astype(bf),
        "w2": p["w2"][l].astype(bf),
    }


def _layer(h, moe, halo, lw, *, tl, n_real, pos0):
    h, su, spb, scx, xs, slot, blk_combo, blk_nvalid, nused = _mixer(
        h, moe, halo, lw, tl=tl, n_real=n_real, pos0=pos0)
    out = _experts(xs, blk_combo, blk_nvalid, nused, lw)
    return h, (slot[:, 0, :].reshape(-1), out), (su[0], spb[0], scx[0])


def kernel(x, meta, norm_mix, w_in, conv_w, conv_b, conv_ln_g, conv_ln_b, w_conv_out, pool_w,
           pool_scale, sc_w, w_sc_out, w_out, norm_ffn, w_group, b_group, w_router, b_router,
           w1, w3, w2, final_norm):
    p = dict(norm_mix=norm_mix, w_in=w_in, conv_w=conv_w, conv_b=conv_b, conv_ln_g=conv_ln_g,
             conv_ln_b=conv_ln_b, w_conv_out=w_conv_out, pool_w=pool_w, pool_scale=pool_scale,
             sc_w=sc_w, w_sc_out=w_sc_out, w_out=w_out, norm_ffn=norm_ffn, w_group=w_group,
             b_group=b_group, w_router=w_router, b_router=b_router, w1=w1, w3=w3, w2=w2)
    bsz, seq, _ = x.shape
    n_tok = bsz * seq
    n_layers = norm_mix.shape[0]
    lws = [_layer_weights(p, l) for l in range(n_layers)]
    zero_halo = (jnp.zeros((HALO_U, D_CONV), jnp.float32),
                 jnp.zeros((HALO_PB, D_POOL), jnp.float32),
                 jnp.zeros((HALO_CX, D_SC), jnp.float32))

    halos = []
    hm = jnp.zeros((1, META_TILE, D_MODEL), jnp.float32).at[0, :N_META].set(meta)
    moe_m = None
    for l, lw in enumerate(lws):
        if l + 1 < n_layers:
            hm, moe_m, halo = _layer(hm, moe_m, zero_halo, lw, tl=META_TILE, n_real=N_META, pos0=0)
        else:
            halo = _mixer(hm, moe_m, zero_halo, lw, tl=META_TILE, n_real=N_META, pos0=0)[1:4]
            halo = tuple(a[0] for a in halo)
        halos.append(halo)

    h = x
    moe = None
    for l, lw in enumerate(lws):
        h, moe, _ = _layer(h, moe, halos[l], lw, tl=MAIN_TILE, n_real=MAIN_TILE, pos0=N_META)
    out = _final_norm(h.reshape(n_tok, D_MODEL), moe[0], moe[1], final_norm[None, :], MAIN_TILE)
    return out.reshape(bsz, seq, D_MODEL)
```

```python
import functools

import numpy as np
import jax
import jax.numpy as jnp
from jax import lax
from jax.experimental import pallas as pl
from jax.experimental.pallas import tpu as pltpu

D_MODEL = 1024
N_META = 16
D_CONV = 512
CONV_K = 31
D_POOL = 512
POOL_WINDOWS = (2, 4, 8, 16)
POOL_GC = 128
POOL_OUT = 256
D_SC = 512
SC_K = 3
N_GROUPS = 4
EPG = 8
N_EXPERTS = 32
D_EXPERT = 256
EPS = 1e-6

C_ALIN, C_AGATE, C_PB, C_CB, C_CC, C_CX, C_GA, C_GB, C_GC = (
    0, 512, 1024, 1536, 2048, 2560, 3072, 4096, 5120)

LANES = 128
SUBLANES = 8
N_PAIRS = EPG * (EPG - 1) // 2
N_COMBO = N_GROUPS * N_PAIRS
ROUTE_E0 = 4
NO_COMBO = LANES - 1

HALO_U = 32
HALO_PB = 16
HALO_CX = 8
W_BLOCK = 512
GATE_BLOCK = W_BLOCK
ROW_CHUNK = 32

BLK = 256
XS_W = D_MODEL + LANES
LANE_WA, LANE_WB = D_MODEL + 1, D_MODEL + 2
MAIN_TILE = 512
META_TILE = 128
VMEM_LIMIT = 56 * 1024 * 1024

_PAIR_OFF = np.cumsum([0] + [EPG - 1 - a for a in range(EPG - 1)])[:-1].astype(np.int32)


def _sigmoid(x):
    return 0.5 * jnp.tanh(0.5 * x) + 0.5


def _rms(x, g):
    return x * lax.rsqrt(jnp.mean(x * x, axis=-1, keepdims=True) + EPS) * g


def _dot(a, b):
    return jnp.dot(a, b, preferred_element_type=jnp.float32)


def _num_blocks(n_tok):
    return n_tok // BLK + min(n_tok, N_COMBO)


def _route(logits):
    n = logits.shape[0]
    lane = lax.broadcasted_iota(jnp.int32, (n, LANES), 1)
    neg = jnp.float32(-jnp.inf)

    def first_max(vals):
        m = jnp.max(vals, axis=-1, keepdims=True)
        idx = jnp.min(jnp.where(vals == m, lane, LANES), axis=-1, keepdims=True)
        return m, idx

    is_g = lane < N_GROUPS
    lg = jnp.where(is_g, logits, neg)
    mg, g_sel = first_max(lg)
    p_sel = 1.0 / jnp.sum(jnp.where(is_g, jnp.exp(lg - mg), 0.0), axis=-1, keepdims=True)

    lo = ROUTE_E0 + EPG * g_sel
    le = jnp.where((lane >= lo) & (lane < lo + EPG), logits, neg)
    v1, i1 = first_max(le)
    v2, i2 = first_max(jnp.where(lane == i1, neg, le))
    t = jnp.exp(v2 - v1)
    w1 = p_sel / (1.0 + t)
    w2 = p_sel * t / (1.0 + t)

    e1 = i1 - lo
    e2 = i2 - lo
    a = jnp.minimum(e1, e2)
    b = jnp.maximum(e1, e2)
    wa = jnp.where(e1 < e2, w1, w2)
    wb = jnp.where(e1 < e2, w2, w1)
    pair = a * (EPG - 1) - ((a * (a - 1)) >> 1) + (b - a - 1)
    return g_sel * N_PAIRS + pair, wa, wb


def _mixer_kernel(*refs, tl, n_real, pos0, has_moe, n_steps, n_tok, n_blocks):
    it = iter(refs)
    slot_prev = next(it) if has_moe else None
    h_ref = next(it)
    moe_hbm = next(it) if has_moe else None
    (hu_ref, hpb_ref, hcx_ref, nmix_ref, w_in_ref, conv_w_ref, conv_b_ref, ln_g_ref, ln_b_ref,
     w_co_ref, pool_w_ref, pool_s_ref, sc_w_ref, w_so_ref, w_out_ref, nffn_ref, w_r_ref, b_r_ref,
     tri_ref, sup_ref,
     hout_ref, su_ref, spb_ref, scx_ref, xs_hbm, slot_ref, blk_combo, blk_nvalid, nused_ref,
     u_s, u_sh, pb_s, cx_s, cb_s, gate_s, act_a, act_b, act_c, stage, meta_v, meta_s, alloc, alloc_iv,
     cnt_s, cur_s, nf_s, sem) = [next(it) for _ in range(47)]
    moe_buf = next(it) if has_moe else None

    j = pl.program_id(1)
    s = pl.program_id(0) * pl.num_programs(1) + j
    dump_row0 = n_blocks * BLK
    SEM_SCATTER, SEM_META, SEM_FLUSH, SEM_GATHER = 0, 1, 2, 3
    ROW_CNT, ROW_CUR, ROW_NF = 0, 1, 2
    LANE_NNEW = LANES - 1

    def meta_copy():
        return pltpu.make_async_copy(meta_v.at[0], meta_s, sem.at[SEM_META])

    def gather_copy(tile, i):
        t = jnp.minimum(tile * tl + i, n_tok - 1)
        return pltpu.make_async_copy(
            moe_hbm.at[pl.ds(slot_prev[t], 1), :], moe_buf.at[tile % 2, pl.ds(i, 1), :], sem.at[SEM_GATHER])

    def gather_wait():
        pltpu.make_async_copy(moe_hbm.at[pl.ds(0, tl), :], moe_buf.at[0], sem.at[SEM_GATHER]).wait()

    def scatter_row(i, slot, par):
        pltpu.make_async_copy(stage.at[par, pl.ds(i, 1), :], xs_hbm.at[pl.ds(slot, 1), :],
                              sem.at[SEM_SCATTER]).start()

    def scatter_wait():
        pltpu.make_async_copy(stage.at[0], xs_hbm.at[pl.ds(0, tl), :], sem.at[SEM_SCATTER]).wait()

    def register_new_blocks():
        first = nf_s[0]
        n_new = meta_s[tl + LANE_NNEW]

        def body(o, c):
            blk_combo[first + o] = meta_s[tl + o]
            blk_nvalid[first + o] = BLK
            return c
        lax.fori_loop(0, n_new, body, 0)
        nf_s[0] = first + n_new

    @pl.when(s == 0)
    def _():
        def unused(i, c):
            blk_combo[i] = NO_COMBO
            blk_nvalid[i] = 0
            return c
        lax.fori_loop(0, n_blocks + 1, unused, 0)
        nf_s[0] = 0
        alloc[...] = jnp.zeros_like(alloc)
        meta_v[...] = jnp.zeros_like(meta_v)
        meta_copy().start()
        stage[...] = jnp.zeros_like(stage)
        u_s[HALO_U + tl:HALO_U + tl + SUBLANES, :] = jnp.zeros((SUBLANES, D_CONV), jnp.float32)
        if has_moe:
            def first(i, c):
                gather_copy(0, i).start()
                return c
            lax.fori_loop(0, tl, first, 0)
            gather_wait()

    @pl.when(j == 0)
    def _():
        u_s[0:HALO_U, :] = hu_ref[...]
        pb_s[0:HALO_PB, :] = hpb_ref[...]
        cx_s[0:HALO_CX, :] = hcx_ref[...]

    meta_copy().wait()
    register_new_blocks()

    prev_par = (s + 1) % 2
    cur_par = s % 2
    for i in range(tl):
        scatter_row(i, jnp.where(s > 0, meta_s[i], dump_row0 + i), prev_par)

    if has_moe:
        hin = h_ref[0] + moe_buf[cur_par]
        for i in range(tl):
            gather_copy(s + 1, i).start()
    else:
        hin = h_ref[0]
    xb = _rms(hin, nmix_ref[...]).astype(jnp.bfloat16)

    def proj(c0):
        return _dot(xb, w_in_ref[c0 // W_BLOCK])

    u_s[HALO_U:HALO_U + tl, :] = proj(C_ALIN) * (jnp.tanh(proj(C_AGATE)) + 1.0)
    pb_s[HALO_PB:HALO_PB + tl, :] = proj(C_PB)
    cx_s[HALO_CX:HALO_CX + tl, :] = proj(C_CC) * proj(C_CX)
    cb_s[...] = proj(C_CB)
    for lo in range(1, SUBLANES):
        u_sh[lo - 1] = u_s[lo:lo + HALO_U + tl, :]

    rc = min(ROW_CHUNK, tl)
    row0 = pos0 + j * tl

    n_chunks = tl // rc
    n_gate_blocks = 3 * D_MODEL // GATE_BLOCK

    for ci, base in enumerate(range(0, tl, rc)):
        for gb in range(ci * n_gate_blocks // n_chunks, (ci + 1) * n_gate_blocks // n_chunks):
            c0 = gb * GATE_BLOCK
            gate_s[:, c0:c0 + GATE_BLOCK] = jnp.tanh(proj(C_GA + c0)) + 1.0
        acc = jnp.broadcast_to(conv_b_ref[...], (rc, D_CONV))
        for k in range(CONV_K):
            hi, lo = divmod(HALO_U - (CONV_K - 1) + k, SUBLANES)
            src = u_s if lo == 0 else u_sh.at[lo - 1]
            w_k = jnp.concatenate([conv_w_ref[k]] * (rc // SUBLANES), axis=0)
            acc = acc + w_k * src[pl.ds(base + SUBLANES * hi, rc), :]
        mu = jnp.mean(acc, axis=-1, keepdims=True)
        dev = acc - mu
        var = jnp.mean(dev * dev, axis=-1, keepdims=True)
        yn = dev * lax.rsqrt(var + EPS) * ln_g_ref[...] + ln_b_ref[...]
        act_a[pl.ds(base, rc), :] = (yn * _sigmoid(yn)).astype(jnp.bfloat16)
        for g, w in enumerate(POOL_WINDOWS):
            cols = slice(g * POOL_GC, (g + 1) * POOL_GC)
            v = pb_s[pl.ds(base + HALO_PB, rc), cols]
            sm = v
            for i in range(1, w):
                sm = sm + pb_s[pl.ds(base + (HALO_PB - i), rc), cols]
            if pos0 + 1 >= w:
                d = sm * (1.0 / w) - v
            else:
                pos = row0 + base + lax.broadcasted_iota(jnp.int32, (rc, POOL_GC), 0)
                d = sm / jnp.minimum(pos + 1, w).astype(jnp.float32) - v
            act_b[pl.ds(base, rc), cols] = d.astype(jnp.bfloat16)
        y = sc_w_ref[0:1, :] * cx_s[pl.ds(base + (HALO_CX - 2), rc), :]
        y = y + sc_w_ref[1:2, :] * cx_s[pl.ds(base + (HALO_CX - 1), rc), :]
        y = y + sc_w_ref[2:3, :] * cx_s[pl.ds(base + HALO_CX, rc), :]
        act_c[pl.ds(base, rc), :] = (cb_s[pl.ds(base, rc), :] * y).astype(jnp.bfloat16)

    new_u = u_s[n_real:n_real + HALO_U, :]
    new_pb = pb_s[n_real:n_real + HALO_PB, :]
    new_cx = cx_s[n_real:n_real + HALO_CX, :]
    u_s[0:HALO_U, :] = new_u
    pb_s[0:HALO_PB, :] = new_pb
    cx_s[0:HALO_CX, :] = new_cx
    su_ref[0] = new_u
    spb_ref[0] = new_pb
    scx_ref[0] = new_cx

    n_wb = D_MODEL // W_BLOCK
    groups_per_wb = W_BLOCK // POOL_OUT
    merged = []
    for nb in range(n_wb):
        cols = slice(nb * W_BLOCK, (nb + 1) * W_BLOCK)
        y_b = jnp.concatenate(
            [_dot(act_b[:, g * POOL_GC:(g + 1) * POOL_GC], pool_w_ref[g])
             for g in range(nb * groups_per_wb, (nb + 1) * groups_per_wb)], axis=-1) * pool_s_ref[:, cols]
        m = gate_s[:, cols] * _dot(act_a[...], w_co_ref[nb])
        m = m + gate_s[:, D_MODEL + nb * W_BLOCK:D_MODEL + (nb + 1) * W_BLOCK] * y_b
        m = m + gate_s[:, 2 * D_MODEL + nb * W_BLOCK:2 * D_MODEL + (nb + 1) * W_BLOCK] * _dot(
            act_c[...], w_so_ref[nb])
        merged.append(m.astype(jnp.bfloat16))
    hout = hin + jnp.concatenate(
        [sum(_dot(merged[kb], w_out_ref[kb, nb]) for kb in range(n_wb)) for nb in range(n_wb)], axis=-1)
    hout_ref[0] = hout

    xn = _rms(hout, nffn_ref[...])
    combo, wa, wb = _route(_dot(xn.astype(jnp.bfloat16), w_r_ref[...]) + b_r_ref[...])

    lane = lax.broadcasted_iota(jnp.int32, (tl, LANES), 1)
    onehot = lane == combo
    if n_real < tl:
        row_id = lax.broadcasted_iota(jnp.int32, (tl, 1), 0)
        onehot = onehot & (row_id < n_real)
    prefix = _dot(tri_ref[...], onehot.astype(jnp.bfloat16))
    n_tile = prefix[tl - 1:tl, :]
    cnt_v = alloc[ROW_CNT:ROW_CNT + 1, :]
    cur_v = alloc[ROW_CUR:ROW_CUR + 1, :]
    nf_v = alloc[ROW_NF:ROW_NF + 1, :]
    inv_blk = 1.0 / BLK

    def pick(vec):
        return jnp.sum(jnp.where(onehot, vec, 0.0), axis=-1, keepdims=True)

    rank = pick(prefix + cnt_v) - 1.0
    full_before = jnp.floor((cnt_v + (BLK - 1)) * inv_blk)
    n_new = jnp.floor((cnt_v + n_tile + (BLK - 1)) * inv_blk) - full_before
    new_excl = _dot(jnp.broadcast_to(n_new, (8, LANES)).astype(jnp.bfloat16), sup_ref[...])[0:1, :]
    new_base = nf_v + new_excl
    chunk = jnp.floor(rank * inv_blk)
    blk = jnp.where(chunk < pick(full_before), pick(cur_v), pick(new_base - full_before) + chunk)
    slot = blk * BLK + (rank - chunk * BLK)
    if n_real < tl:
        slot = jnp.where(row_id < n_real, slot, (dump_row0 + row_id).astype(jnp.float32))
    n_new_total = jnp.sum(n_new, axis=-1, keepdims=True)
    alloc[ROW_CNT:ROW_CNT + 1, :] = cnt_v + n_tile
    alloc[ROW_CUR:ROW_CUR + 1, :] = jnp.where(n_new > 0.0, new_base + n_new - 1.0, cur_v)
    alloc[ROW_NF:ROW_NF + 1, :] = nf_v + n_new_total

    sq_sub = lax.broadcasted_iota(jnp.int32, (LANES, LANES), 0)
    sq_lane = lax.broadcasted_iota(jnp.int32, (LANES, LANES), 1)
    new_incl_col = jnp.where(sq_sub == 0, jnp.broadcast_to(new_excl + n_new, (LANES, LANES)), 0.0).T[:, 0:1]
    new_combo = jnp.sum(jnp.where(new_incl_col <= sq_lane.astype(jnp.float32), 1.0, 0.0),
                        axis=0, keepdims=True)
    lane_row = lax.broadcasted_iota(jnp.int32, (1, LANES), 1)
    new_combo = jnp.where(lane_row == LANE_NNEW, n_new_total, new_combo)

    info = jnp.where(lane == 0, slot,
                     jnp.where(lane == LANE_WA - D_MODEL, wa,
                               jnp.where(lane == LANE_WB - D_MODEL, wb, 0.0)))
    scatter_wait()
    if has_moe:
        gather_wait()
    stage[cur_par, :, 0:D_MODEL] = xn
    stage[cur_par, :, D_MODEL:XS_W] = info
    slot_rows = info.T[0:8, :].astype(jnp.int32)
    if n_real < tl:
        slot_ref[0] = jnp.where(lax.broadcasted_iota(jnp.int32, (8, tl), 1) < n_real, slot_rows, 0)
    else:
        slot_ref[0] = slot_rows
    meta_v[:, 0:tl] = slot_rows
    meta_v[:, tl:tl + LANES] = jnp.broadcast_to(new_combo, (8, LANES)).astype(jnp.int32)
    meta_copy().start()

    @pl.when(s == n_steps - 1)
    def _():
        meta_copy().wait()
        register_new_blocks()

        def flush(i, c):
            scatter_row(i, meta_s[i], cur_par)
            return c
        lax.fori_loop(0, tl, flush, 0)
        scatter_wait()

        alloc_iv[...] = alloc[...].astype(jnp.int32)
        c0 = pltpu.make_async_copy(alloc_iv.at[ROW_CNT], cnt_s, sem.at[SEM_FLUSH])
        c1 = pltpu.make_async_copy(alloc_iv.at[ROW_CUR], cur_s, sem.at[SEM_FLUSH])
        c0.start()
        c1.start()
        c0.wait()
        c1.wait()

        def close(c, carry):
            n = cnt_s[c]
            bidx = jnp.where(n > 0, cur_s[c], n_blocks)
            blk_nvalid[bidx] = ((n - 1) & (BLK - 1)) + 1
            return carry
        lax.fori_loop(0, LANES, close, 0)
        nused_ref[0] = nf_s[0]


def _const_spec(shape):
    zeros = (0,) * len(shape)
    return pl.BlockSpec(shape, lambda b, j, *_: zeros, pipeline_mode=pl.Buffered(1))


def _mixer(h, moe, halo, lw, *, tl, n_real, pos0):
    bsz, seq, _ = h.shape
    nt = seq // tl
    n_tok = bsz * seq
    n_blocks = _num_blocks(bsz * nt * n_real)
    has_moe = moe is not None
    tile = pl.BlockSpec((1, tl, D_MODEL), lambda b, j, *_: (b, j, 0))
    smem = pl.BlockSpec(memory_space=pltpu.SMEM)
    consts = [halo[0], halo[1], halo[2], lw["norm_mix"], lw["w_in"], lw["conv_w"], lw["conv_b"],
              lw["conv_ln_g"], lw["conv_ln_b"], lw["w_conv_out"], lw["pool_w"], lw["pool_scale"],
              lw["sc_w"], lw["w_sc_out"], lw["w_out"], lw["norm_ffn"], lw["w_route"], lw["b_route"],
              jnp.tril(jnp.ones((tl, tl), jnp.bfloat16)),
              jnp.triu(jnp.ones((LANES, LANES), jnp.bfloat16), 1)]
    in_specs = ([tile] + ([pl.BlockSpec(memory_space=pl.ANY)] if has_moe else [])
                + [_const_spec(c.shape) for c in consts])
    out_shape = (
        jax.ShapeDtypeStruct((bsz, seq, D_MODEL), jnp.float32),
        jax.ShapeDtypeStruct((bsz, HALO_U, D_CONV), jnp.float32),
        jax.ShapeDtypeStruct((bsz, HALO_PB, D_POOL), jnp.float32),
        jax.ShapeDtypeStruct((bsz, HALO_CX, D_SC), jnp.float32),
        jax.ShapeDtypeStruct((n_blocks * BLK + tl, XS_W), jnp.float32),
        jax.ShapeDtypeStruct((bsz * nt, 8, tl), jnp.int32),
        jax.ShapeDtypeStruct((n_blocks + 1,), jnp.int32),
        jax.ShapeDtypeStruct((n_blocks + 1,), jnp.int32),
        jax.ShapeDtypeStruct((1,), jnp.int32),
    )
    out_specs = (
        tile,
        pl.BlockSpec((1, HALO_U, D_CONV), lambda b, j, *_: (b, 0, 0)),
        pl.BlockSpec((1, HALO_PB, D_POOL), lambda b, j, *_: (b, 0, 0)),
        pl.BlockSpec((1, HALO_CX, D_SC), lambda b, j, *_: (b, 0, 0)),
        pl.BlockSpec(memory_space=pl.ANY),
        pl.BlockSpec((1, 8, tl), lambda b, j, *_: (b * nt + j, 0, 0)),
        smem, smem, smem,
    )
    scratch = [
        pltpu.VMEM((HALO_U + tl + SUBLANES, D_CONV), jnp.float32),
        pltpu.VMEM((SUBLANES - 1, HALO_U + tl, D_CONV), jnp.float32),
        pltpu.VMEM((HALO_PB + tl, D_POOL), jnp.float32),
        pltpu.VMEM((HALO_CX + tl, D_SC), jnp.float32),
        pltpu.VMEM((tl, D_SC), jnp.float32),
        pltpu.VMEM((tl, 3 * D_MODEL), jnp.float32),
        pltpu.VMEM((tl, D_CONV), jnp.bfloat16),
        pltpu.VMEM((tl, D_POOL), jnp.bfloat16),
        pltpu.VMEM((tl, D_SC), jnp.bfloat16),
        pltpu.VMEM((2, tl, XS_W), jnp.float32),
        pltpu.VMEM((8, tl + LANES), jnp.int32),
        pltpu.SMEM((tl + LANES,), jnp.int32),
        pltpu.VMEM((8, LANES), jnp.float32),
        pltpu.VMEM((8, LANES), jnp.int32),
        pltpu.SMEM((LANES,), jnp.int32),
        pltpu.SMEM((LANES,), jnp.int32),
        pltpu.SMEM((1,), jnp.int32),
        pltpu.SemaphoreType.DMA((6,)),
    ]
    if has_moe:
        scratch.append(pltpu.VMEM((2, tl, D_MODEL), jnp.float32))
    args = ([moe[0]] if has_moe else []) + [h] + ([moe[1]] if has_moe else []) + consts
    grid_spec = pltpu.PrefetchScalarGridSpec(
        num_scalar_prefetch=1 if has_moe else 0,
        grid=(bsz, nt), in_specs=in_specs, out_specs=out_specs, scratch_shapes=scratch)
    return pl.pallas_call(
        functools.partial(_mixer_kernel, tl=tl, n_real=n_real, pos0=pos0, has_moe=has_moe,
                          n_steps=bsz * nt, n_tok=n_tok, n_blocks=n_blocks),
        grid_spec=grid_spec,
        out_shape=out_shape,
        compiler_params=pltpu.CompilerParams(
            dimension_semantics=("arbitrary", "arbitrary"), vmem_limit_bytes=VMEM_LIMIT),
        name="mixer",
    )(*args)


def _experts_kernel(order_ref, be1_ref, be2_ref, nvalid_ref, nused_ref, xs_ref,
                    w1a_ref, w3a_ref, w2a_ref, w1b_ref, w3b_ref, w2b_ref, out_ref):
    i = pl.program_id(0)

    @pl.when(i < nused_ref[0])
    def _():
        rows = xs_ref[...]
        ok = lax.broadcasted_iota(jnp.int32, (BLK, 1), 0) < nvalid_ref[i]
        x = jnp.where(ok, rows[:, 0:D_MODEL], 0.0).astype(jnp.bfloat16)
        wa = jnp.where(ok, rows[:, LANE_WA:LANE_WA + 1], 0.0)
        wb = jnp.where(ok, rows[:, LANE_WB:LANE_WB + 1], 0.0)

        def expert(w1_ref, w3_ref, w2_ref):
            hid = jax.nn.silu(_dot(x, w1_ref[0])) * _dot(x, w3_ref[0])
            return _dot(hid.astype(jnp.bfloat16), w2_ref[0])

        out_ref[...] = (expert(w1a_ref, w3a_ref, w2a_ref) * wa
                        + expert(w1b_ref, w3b_ref, w2b_ref) * wb)


def _experts(xs, blk_combo, blk_nvalid, nused, lw):
    n_blocks = blk_combo.shape[0] - 1
    ids = jnp.arange(n_blocks, dtype=jnp.int32)
    fill_bits = BLK.bit_length() - 1
    id_bits = max(n_blocks - 1, 1).bit_length()
    key = ((blk_combo[:n_blocks] << (id_bits + fill_bits)) | (ids << fill_bits)
           | jnp.clip(blk_nvalid[:n_blocks] - 1, 0, BLK - 1))
    key = jnp.sort(key)
    last = lax.dynamic_slice(key, (jnp.maximum(nused[0] - 1, 0),), (1,))
    key = jnp.where(ids < nused[0], key, last)
    combo = key >> (id_bits + fill_bits)
    order = (key >> fill_bits) & ((1 << id_bits) - 1)
    nvalid = (key & (BLK - 1)) + 1
    grp = combo // N_PAIRS
    pair = combo % N_PAIRS
    a = jnp.sum(pair[:, None] >= jnp.asarray(_PAIR_OFF)[None, 1:], axis=1).astype(jnp.int32)
    off_a = a * (EPG - 1) - ((a * (a - 1)) >> 1)
    be1 = jnp.minimum(grp * EPG + a, N_EXPERTS - 1).astype(jnp.int32)
    be2 = jnp.minimum(grp * EPG + pair - off_a + a + 1, N_EXPERTS - 1).astype(jnp.int32)

    def wspec(which):
        def index_map(i, order_r, be1_r, be2_r, nv_r, nu_r):
            return ((be1_r, be2_r)[which][i], 0, 0)
        return index_map

    def blk_map(i, order_r, *_):
        return (order_r[i], 0)

    up = (1, D_MODEL, D_EXPERT)
    down = (1, D_EXPERT, D_MODEL)
    grid_spec = pltpu.PrefetchScalarGridSpec(
        num_scalar_prefetch=5,
        grid=(n_blocks,),
        in_specs=[
            pl.BlockSpec((BLK, XS_W), blk_map),
            pl.BlockSpec(up, wspec(0)), pl.BlockSpec(up, wspec(0)), pl.BlockSpec(down, wspec(0)),
            pl.BlockSpec(up, wspec(1)), pl.BlockSpec(up, wspec(1)), pl.BlockSpec(down, wspec(1)),
        ],
        out_specs=pl.BlockSpec((BLK, D_MODEL), blk_map),
    )
    return pl.pallas_call(
        _experts_kernel,
        grid_spec=grid_spec,
        out_shape=jax.ShapeDtypeStruct((n_blocks * BLK, D_MODEL), jnp.float32),
        compiler_params=pltpu.CompilerParams(
            dimension_semantics=("arbitrary",), vmem_limit_bytes=VMEM_LIMIT),
        name="experts",
    )(order.astype(jnp.int32), be1, be2, nvalid.astype(jnp.int32), nused, xs,
      lw["w1"], lw["w3"], lw["w2"], lw["w1"], lw["w3"], lw["w2"])


def _final_kernel(slot_ref, h_ref, moe_hbm, g_ref, out_ref, moe_buf, sem, *, rows, n_steps):
    i = pl.program_id(0)

    n_tok = rows * n_steps

    def row_copy(tile, r):
        t = jnp.minimum(tile * rows + r, n_tok - 1)
        return pltpu.make_async_copy(moe_hbm.at[pl.ds(slot_ref[t], 1), :],
                                     moe_buf.at[tile % 3, pl.ds(r, 1), :], sem.at[tile % 3])

    def wait_tile(tile):
        pltpu.make_async_copy(moe_hbm.at[pl.ds(0, rows), :], moe_buf.at[0], sem.at[tile % 3]).wait()

    @pl.when(i == 0)
    def _():
        def body(r, c):
            row_copy(0, r).start()
            row_copy(1, r).start()
            return c
        lax.fori_loop(0, rows, body, 0)
        wait_tile(0)

    for r in range(rows):
        row_copy(i + 2, r).start()
    out_ref[...] = _rms(h_ref[...] + moe_buf[i % 3], g_ref[...])
    wait_tile(i + 1)

    @pl.when(i == n_steps - 1)
    def _():
        wait_tile(i + 2)


def _final_norm(h, slot, moe_rows, g, rows):
    n = h.shape[0]
    tile = pl.BlockSpec((rows, D_MODEL), lambda i, *_: (i, 0))
    grid_spec = pltpu.PrefetchScalarGridSpec(
        num_scalar_prefetch=1,
        grid=(n // rows,),
        in_specs=[tile, pl.BlockSpec(memory_space=pl.ANY),
                  pl.BlockSpec((1, D_MODEL), lambda i, *_: (0, 0))],
        out_specs=tile,
        scratch_shapes=[pltpu.VMEM((3, rows, D_MODEL), jnp.float32), pltpu.SemaphoreType.DMA((3,))],
    )
    return pl.pallas_call(
        functools.partial(_final_kernel, rows=rows, n_steps=n // rows),
        grid_spec=grid_spec,
        out_shape=jax.ShapeDtypeStruct((n, D_MODEL), jnp.float32),
        compiler_params=pltpu.CompilerParams(
            dimension_semantics=("arbitrary",), vmem_limit_bytes=VMEM_LIMIT),
        name="final_norm",
    )(slot, h, moe_rows, g)


def _layer_weights(p, l):
    bf = jnp.bfloat16
    w_route = jnp.zeros((D_MODEL, LANES), jnp.float32)
    w_route = w_route.at[:, :N_GROUPS].set(p["w_group"][l])
    w_route = w_route.at[:, ROUTE_E0:ROUTE_E0 + N_EXPERTS].set(p["w_router"][l])
    b_route = jnp.zeros((1, LANES), jnp.float32)
    b_route = b_route.at[0, :N_GROUPS].set(p["b_group"][l])
    b_route = b_route.at[0, ROUTE_E0:ROUTE_E0 + N_EXPERTS].set(p["b_router"][l])

    def col_blocks(w):
        k, n = w.shape
        return w.reshape(k, n // W_BLOCK, W_BLOCK).transpose(1, 0, 2).astype(bf)

    col = jnp.arange(p["w_in"].shape[-1])
    halved = ((col < C_PB) | (col >= C_GA))
    w_in = p["w_in"][l] * jnp.where(halved, 0.5, 1.0)[None, :]
    w_out = (0.5 * p["w_out"][l]).reshape(D_MODEL // W_BLOCK, W_BLOCK, D_MODEL // W_BLOCK, W_BLOCK)
    return {
        "norm_mix": p["norm_mix"][l][None, :],
        "w_in": col_blocks(w_in),
        "conv_w": jnp.broadcast_to(p["conv_w"][l][:, None, :], (CONV_K, SUBLANES, D_CONV)),
        "conv_b": p["conv_b"][l][None, :],
        "conv_ln_g": p["conv_ln_g"][l][None, :],
        "conv_ln_b": p["conv_ln_b"][l][None, :],
        "w_conv_out": col_blocks(p["w_conv_out"][l]),
        "pool_w": p["pool_w"][l].astype(bf),
        "pool_scale": p["pool_scale"][l][None, :],
        "sc_w": p["sc_w"][l],
        "w_sc_out": col_blocks(p["w_sc_out"][l]),
        "w_out": w_out.transpose(0, 2, 1, 3).astype(bf),
        "norm_ffn": p["norm_ffn"][l][None, :],
        "w_route": w_route.astype(bf),
        "b_route": b_route,
        "w1": p["w1"][l].astype(bf),
        "w3": p["w3"][l].astype(bf),
        "w2": p["w2"][l].astype(bf),
    }


def _layer(h, moe, halo, lw, *, tl, n_real, pos0):
    h, su, spb, scx, xs, slot, blk_combo, blk_nvalid, nused = _mixer(
        h, moe, halo, lw, tl=tl, n_real=n_real, pos0=pos0)
    out = _experts(xs, blk_combo, blk_nvalid, nused, lw)
    return h, (slot[:, 0, :].reshape(-1), out), (su[0], spb[0], scx[0])


def kernel(x, meta, norm_mix, w_in, conv_w, conv_b, conv_ln_g, conv_ln_b, w_conv_out, pool_w,
           pool_scale, sc_w, w_sc_out, w_out, norm_ffn, w_group, b_group, w_router, b_router,
           w1, w3, w2, final_norm):
    p = dict(norm_mix=norm_mix, w_in=w_in, conv_w=conv_w, conv_b=conv_b, conv_ln_g=conv_ln_g,
             conv_ln_b=conv_ln_b, w_conv_out=w_conv_out, pool_w=pool_w, pool_scale=pool_scale,
             sc_w=sc_w, w_sc_out=w_sc_out, w_out=w_out, norm_ffn=norm_ffn, w_group=w_group,
             b_group=b_group, w_router=w_router, b_router=b_router, w1=w1, w3=w3, w2=w2)
    bsz, seq, _ = x.shape
    n_tok = bsz * seq
    n_layers = norm_mix.shape[0]
    lws = [_layer_weights(p, l) for l in range(n_layers)]
    zero_halo = (jnp.zeros((HALO_U, D_CONV), jnp.float32),
                 jnp.zeros((HALO_PB, D_POOL), jnp.float32),
                 jnp.zeros((HALO_CX, D_SC), jnp.float32))

    halos = []
    hm = jnp.zeros((1, META_TILE, D_MODEL), jnp.float32).at[0, :N_META].set(meta)
    moe_m = None
    for l, lw in enumerate(lws):
        if l + 1 < n_layers:
            hm, moe_m, halo = _layer(hm, moe_m, zero_halo, lw, tl=META_TILE, n_real=N_META, pos0=0)
        else:
            halo = _mixer(hm, moe_m, zero_halo, lw, tl=META_TILE, n_real=N_META, pos0=0)[1:4]
            halo = tuple(a[0] for a in halo)
        halos.append(halo)

    h = x
    moe = None
    for l, lw in enumerate(lws):
        h, moe, _ = _layer(h, moe, halos[l], lw, tl=MAIN_TILE, n_real=MAIN_TILE, pos0=N_META)
    out = _final_norm(h.reshape(n_tok, D_MODEL), moe[0], moe[1], final_norm[None, :], MAIN_TILE)
    return out.reshape(bsz, seq, D_MODEL)
```

```python
import functools

import numpy as np
import jax
import jax.numpy as jnp
from jax import lax
from jax.experimental import pallas as pl
from jax.experimental.pallas import tpu as pltpu

D_MODEL = 1024
N_META = 16
D_CONV = 512
CONV_K = 31
D_POOL = 512
POOL_WINDOWS = (2, 4, 8, 16)
POOL_GC = 128
POOL_OUT = 256
D_SC = 512
SC_K = 3
N_GROUPS = 4
EPG = 8
N_EXPERTS = 32
D_EXPERT = 256
EPS = 1e-6

C_ALIN, C_AGATE, C_PB, C_CB, C_CC, C_CX, C_GA, C_GB, C_GC = (
    0, 512, 1024, 1536, 2048, 2560, 3072, 4096, 5120)

LANES = 128
SUBLANES = 8
N_PAIRS = EPG * (EPG - 1) // 2
N_COMBO = N_GROUPS * N_PAIRS
ROUTE_E0 = 4
NO_COMBO = LANES - 1

HALO_U = 32
HALO_PB = 16
HALO_CX = 8
W_BLOCK = 512
GATE_BLOCK = W_BLOCK
ROW_CHUNK = 32

BLK = 256
XS_W = D_MODEL + LANES
LANE_WA, LANE_WB = D_MODEL + 1, D_MODEL + 2
MAIN_TILE = 512
META_TILE = 128
VMEM_LIMIT = 56 * 1024 * 1024

_PAIR_OFF = np.cumsum([0] + [EPG - 1 - a for a in range(EPG - 1)])[:-1].astype(np.int32)


def _sigmoid(x):
    return 0.5 * jnp.tanh(0.5 * x) + 0.5


def _rms(x, g):
    return x * lax.rsqrt(jnp.mean(x * x, axis=-1, keepdims=True) + EPS) * g


def _dot(a, b):
    return jnp.dot(a, b, preferred_element_type=jnp.float32)


def _num_blocks(n_tok):
    return n_tok // BLK + min(n_tok, N_COMBO)


def _route(logits):
    n = logits.shape[0]
    lane = lax.broadcasted_iota(jnp.int32, (n, LANES), 1)
    neg = jnp.float32(-jnp.inf)

    def first_max(vals):
        m = jnp.max(vals, axis=-1, keepdims=True)
        idx = jnp.min(jnp.where(vals == m, lane, LANES), axis=-1, keepdims=True)
        return m, idx

    is_g = lane < N_GROUPS
    lg = jnp.where(is_g, logits, neg)
    mg, g_sel = first_max(lg)
    p_sel = 1.0 / jnp.sum(jnp.where(is_g, jnp.exp(lg - mg), 0.0), axis=-1, keepdims=True)

    lo = ROUTE_E0 + EPG * g_sel
    le = jnp.where((lane >= lo) & (lane < lo + EPG), logits, neg)
    v1, i1 = first_max(le)
    v2, i2 = first_max(jnp.where(lane == i1, neg, le))
    t = jnp.exp(v2 - v1)
    w1 = p_sel / (1.0 + t)
    w2 = p_sel * t / (1.0 + t)

    e1 = i1 - lo
    e2 = i2 - lo
    a = jnp.minimum(e1, e2)
    b = jnp.maximum(e1, e2)
    wa = jnp.where(e1 < e2, w1, w2)
    wb = jnp.where(e1 < e2, w2, w1)
    pair = a * (EPG - 1) - ((a * (a - 1)) >> 1) + (b - a - 1)
    return g_sel * N_PAIRS + pair, wa, wb


def _mixer_kernel(*refs, tl, n_real, pos0, has_moe, n_steps, n_tok, n_blocks):
    it = iter(refs)
    slot_prev = next(it) if has_moe else None
    h_ref = next(it)
    moe_hbm = next(it) if has_moe else None
    (hu_ref, hpb_ref, hcx_ref, nmix_ref, w_in_ref, conv_w_ref, conv_b_ref, ln_g_ref, ln_b_ref,
     w_co_ref, pool_w_ref, pool_s_ref, sc_w_ref, w_so_ref, w_out_ref, nffn_ref, w_r_ref, b_r_ref,
     tri_ref, sup_ref,
     hout_ref, su_ref, spb_ref, scx_ref, xs_hbm, slot_ref, blk_combo, blk_nvalid, nused_ref,
     u_s, u_sh, pb_s, cx_s, cb_s, gate_s, act_a, act_b, act_c, stage, meta_v, meta_s, alloc,
     alloc_iv, cnt_s, cur_s, nf_s, sem) = [next(it) for _ in range(47)]
    moe_buf = next(it) if has_moe else None

    j = pl.program_id(1)
    s = pl.program_id(0) * pl.num_programs(1) + j
    dump_row0 = n_blocks * BLK
    SEM_SCATTER, SEM_META, SEM_FLUSH, SEM_GATHER = 0, 1, 2, 3
    ROW_CNT, ROW_CUR, ROW_NF = 0, 1, 2
    LANE_NNEW = LANES - 1

    def meta_copy():
        return pltpu.make_async_copy(meta_v.at[0], meta_s, sem.at[SEM_META])

    def gather_copy(tile, i):
        t = jnp.minimum(tile * tl + i, n_tok - 1)
        return pltpu.make_async_copy(
            moe_hbm.at[pl.ds(slot_prev[t], 1), :], moe_buf.at[tile % 2, pl.ds(i, 1), :], sem.at[SEM_GATHER])

    def gather_wait():
        pltpu.make_async_copy(moe_hbm.at[pl.ds(0, tl), :], moe_buf.at[0], sem.at[SEM_GATHER]).wait()

    def scatter_row(i, slot, par):
        pltpu.make_async_copy(stage.at[par, pl.ds(i, 1), :], xs_hbm.at[pl.ds(slot, 1), :],
                              sem.at[SEM_SCATTER]).start()

    def scatter_wait():
        pltpu.make_async_copy(stage.at[0], xs_hbm.at[pl.ds(0, tl), :], sem.at[SEM_SCATTER]).wait()

    def register_new_blocks():
        first = nf_s[0]
        n_new = meta_s[tl + LANE_NNEW]

        def body(o, c):
            blk_combo[first + o] = meta_s[tl + o]
            blk_nvalid[first + o] = BLK
            return c
        lax.fori_loop(0, n_new, body, 0)
        nf_s[0] = first + n_new

    @pl.when(s == 0)
    def _():
        def unused(i, c):
            blk_combo[i] = NO_COMBO
            blk_nvalid[i] = 0
            return c
        lax.fori_loop(0, n_blocks + 1, unused, 0)
        nf_s[0] = 0
        alloc[...] = jnp.zeros_like(alloc)
        meta_v[...] = jnp.zeros_like(meta_v)
        meta_copy().start()
        stage[...] = jnp.zeros_like(stage)
        u_s[HALO_U + tl:HALO_U + tl + SUBLANES, :] = jnp.zeros((SUBLANES, D_CONV), jnp.float32)
        if has_moe:
            def first(i, c):
                gather_copy(0, i).start()
                return c
            lax.fori_loop(0, tl, first, 0)
            gather_wait()

    @pl.when(j == 0)
    def _():
        u_s[0:HALO_U, :] = hu_ref[...]
        pb_s[0:HALO_PB, :] = hpb_ref[...]
        cx_s[0:HALO_CX, :] = hcx_ref[...]

    meta_copy().wait()
    register_new_blocks()

    prev_par = (s + 1) % 2
    cur_par = s % 2
    for i in range(tl):
        scatter_row(i, jnp.where(s > 0, meta_s[i], dump_row0 + i), prev_par)

    if has_moe:
        hin = h_ref[0] + moe_buf[cur_par]
        for i in range(tl):
            gather_copy(s + 1, i).start()
    else:
        hin = h_ref[0]
    xb = _rms(hin, nmix_ref[...]).astype(jnp.bfloat16)

    def proj(c0):
        return _dot(xb, w_in_ref[c0 // W_BLOCK])

    u_s[HALO_U:HALO_U + tl, :] = proj(C_ALIN) * (jnp.tanh(proj(C_AGATE)) + 1.0)
    pb_s[HALO_PB:HALO_PB + tl, :] = proj(C_PB)
    cx_s[HALO_CX:HALO_CX + tl, :] = proj(C_CC) * proj(C_CX)
    cb_s[...] = proj(C_CB)
    for lo in range(1, SUBLANES):
        u_sh[lo - 1] = u_s[lo:lo + HALO_U + tl, :]

    rc = min(ROW_CHUNK, tl)
    row0 = pos0 + j * tl

    n_chunks = tl // rc
    n_gate_blocks = 3 * D_MODEL // GATE_BLOCK

    for ci, base in enumerate(range(0, tl, rc)):
        for gb in range(ci * n_gate_blocks // n_chunks, (ci + 1) * n_gate_blocks // n_chunks):
            c0 = gb * GATE_BLOCK
            gate_s[:, c0:c0 + GATE_BLOCK] = (jnp.tanh(proj(C_GA + c0)) + 1.0).astype(jnp.bfloat16)
        acc = jnp.broadcast_to(conv_b_ref[...], (rc, D_CONV))
        for k in range(CONV_K):
            hi, lo = divmod(HALO_U - (CONV_K - 1) + k, SUBLANES)
            src = u_s if lo == 0 else u_sh.at[lo - 1]
            w_k = jnp.concatenate([conv_w_ref[k]] * (rc // SUBLANES), axis=0)
            acc = acc + w_k * src[pl.ds(base + SUBLANES * hi, rc), :]
        mu = jnp.mean(acc, axis=-1, keepdims=True)
        dev = acc - mu
        var = jnp.mean(dev * dev, axis=-1, keepdims=True)
        yn = dev * lax.rsqrt(var + EPS) * ln_g_ref[...] + ln_b_ref[...]
        act_a[pl.ds(base, rc), :] = (yn * _sigmoid(yn)).astype(jnp.bfloat16)
        for g, w in enumerate(POOL_WINDOWS):
            cols = slice(g * POOL_GC, (g + 1) * POOL_GC)
            v = pb_s[pl.ds(base + HALO_PB, rc), cols]
            sm = v
            for i in range(1, w):
                sm = sm + pb_s[pl.ds(base + (HALO_PB - i), rc), cols]
            if pos0 + 1 >= w:
                d = sm * (1.0 / w) - v
            else:
                pos = row0 + base + lax.broadcasted_iota(jnp.int32, (rc, POOL_GC), 0)
                d = sm / jnp.minimum(pos + 1, w).astype(jnp.float32) - v
            act_b[pl.ds(base, rc), cols] = d.astype(jnp.bfloat16)
        y = sc_w_ref[0:1, :] * cx_s[pl.ds(base + (HALO_CX - 2), rc), :]
        y = y + sc_w_ref[1:2, :] * cx_s[pl.ds(base + (HALO_CX - 1), rc), :]
        y = y + sc_w_ref[2:3, :] * cx_s[pl.ds(base + HALO_CX, rc), :]
        act_c[pl.ds(base, rc), :] = (cb_s[pl.ds(base, rc), :] * y).astype(jnp.bfloat16)

    new_u = u_s[n_real:n_real + HALO_U, :]
    new_pb = pb_s[n_real:n_real + HALO_PB, :]
    new_cx = cx_s[n_real:n_real + HALO_CX, :]
    u_s[0:HALO_U, :] = new_u
    pb_s[0:HALO_PB, :] = new_pb
    cx_s[0:HALO_CX, :] = new_cx
    su_ref[0] = new_u
    spb_ref[0] = new_pb
    scx_ref[0] = new_cx

    n_wb = D_MODEL // W_BLOCK
    groups_per_wb = W_BLOCK // POOL_OUT
    merged = []
    for nb in range(n_wb):
        cols = slice(nb * W_BLOCK, (nb + 1) * W_BLOCK)
        y_b = jnp.concatenate(
            [_dot(act_b[:, g * POOL_GC:(g + 1) * POOL_GC], pool_w_ref[g])
             for g in range(nb * groups_per_wb, (nb + 1) * groups_per_wb)], axis=-1) * pool_s_ref[:, cols]
        def gate(branch):
            c0 = branch * D_MODEL + nb * W_BLOCK
            return gate_s[:, c0:c0 + W_BLOCK].astype(jnp.float32)

        m = gate(0) * _dot(act_a[...], w_co_ref[nb])
        m = m + gate(1) * y_b
        m = m + gate(2) * _dot(act_c[...], w_so_ref[nb])
        merged.append(m.astype(jnp.bfloat16))
    hout = hin + jnp.concatenate(
        [sum(_dot(merged[kb], w_out_ref[kb, nb]) for kb in range(n_wb)) for nb in range(n_wb)], axis=-1)
    hout_ref[0] = hout

    xn = _rms(hout, nffn_ref[...])
    combo, wa, wb = _route(_dot(xn.astype(jnp.bfloat16), w_r_ref[...]) + b_r_ref[...])

    lane = lax.broadcasted_iota(jnp.int32, (tl, LANES), 1)
    onehot = lane == combo
    if n_real < tl:
        row_id = lax.broadcasted_iota(jnp.int32, (tl, 1), 0)
        onehot = onehot & (row_id < n_real)
    prefix = _dot(tri_ref[...], onehot.astype(jnp.bfloat16))
    n_tile = prefix[tl - 1:tl, :]
    cnt_v = alloc[ROW_CNT:ROW_CNT + 1, :]
    cur_v = alloc[ROW_CUR:ROW_CUR + 1, :]
    nf_v = alloc[ROW_NF:ROW_NF + 1, :]
    inv_blk = 1.0 / BLK

    def pick(vec):
        return jnp.sum(jnp.where(onehot, vec, 0.0), axis=-1, keepdims=True)

    rank = pick(prefix + cnt_v) - 1.0
    full_before = jnp.floor((cnt_v + (BLK - 1)) * inv_blk)
    n_new = jnp.floor((cnt_v + n_tile + (BLK - 1)) * inv_blk) - full_before
    new_excl = _dot(jnp.broadcast_to(n_new, (8, LANES)).astype(jnp.bfloat16), sup_ref[...])[0:1, :]
    new_base = nf_v + new_excl
    chunk = jnp.floor(rank * inv_blk)
    blk = jnp.where(chunk < pick(full_before), pick(cur_v), pick(new_base - full_before) + chunk)
    slot = blk * BLK + (rank - chunk * BLK)
    if n_real < tl:
        slot = jnp.where(row_id < n_real, slot, (dump_row0 + row_id).astype(jnp.float32))
    n_new_total = jnp.sum(n_new, axis=-1, keepdims=True)
    alloc[ROW_CNT:ROW_CNT + 1, :] = cnt_v + n_tile
    alloc[ROW_CUR:ROW_CUR + 1, :] = jnp.where(n_new > 0.0, new_base + n_new - 1.0, cur_v)
    alloc[ROW_NF:ROW_NF + 1, :] = nf_v + n_new_total

    sq_sub = lax.broadcasted_iota(jnp.int32, (LANES, LANES), 0)
    sq_lane = lax.broadcasted_iota(jnp.int32, (LANES, LANES), 1)
    new_incl_col = jnp.where(sq_sub == 0, jnp.broadcast_to(new_excl + n_new, (LANES, LANES)), 0.0).T[:, 0:1]
    new_combo = jnp.sum(jnp.where(new_incl_col <= sq_lane.astype(jnp.float32), 1.0, 0.0),
                        axis=0, keepdims=True)
    lane_row = lax.broadcasted_iota(jnp.int32, (1, LANES), 1)
    new_combo = jnp.where(lane_row == LANE_NNEW, n_new_total, new_combo)

    info = jnp.where(lane == 0, slot,
                     jnp.where(lane == LANE_WA - D_MODEL, wa,
                               jnp.where(lane == LANE_WB - D_MODEL, wb, 0.0)))
    scatter_wait()
    if has_moe:
        gather_wait()
    stage[cur_par, :, 0:D_MODEL] = xn
    stage[cur_par, :, D_MODEL:XS_W] = info
    slot_rows = info.T[0:8, :].astype(jnp.int32)
    if n_real < tl:
        slot_ref[0] = jnp.where(lax.broadcasted_iota(jnp.int32, (8, tl), 1) < n_real, slot_rows, 0)
    else:
        slot_ref[0] = slot_rows
    meta_v[:, 0:tl] = slot_rows
    meta_v[:, tl:tl + LANES] = jnp.broadcast_to(new_combo, (8, LANES)).astype(jnp.int32)
    meta_copy().start()

    @pl.when(s == n_steps - 1)
    def _():
        meta_copy().wait()
        register_new_blocks()

        def flush(i, c):
            scatter_row(i, meta_s[i], cur_par)
            return c
        lax.fori_loop(0, tl, flush, 0)
        scatter_wait()

        alloc_iv[...] = alloc[...].astype(jnp.int32)
        c0 = pltpu.make_async_copy(alloc_iv.at[ROW_CNT], cnt_s, sem.at[SEM_FLUSH])
        c1 = pltpu.make_async_copy(alloc_iv.at[ROW_CUR], cur_s, sem.at[SEM_FLUSH])
        c0.start()
        c1.start()
        c0.wait()
        c1.wait()

        def close(c, carry):
            n = cnt_s[c]
            bidx = jnp.where(n > 0, cur_s[c], n_blocks)
            blk_nvalid[bidx] = ((n - 1) & (BLK - 1)) + 1
            return carry
        lax.fori_loop(0, LANES, close, 0)
        nused_ref[0] = nf_s[0]


def _const_spec(shape):
    zeros = (0,) * len(shape)
    return pl.BlockSpec(shape, lambda b, j, *_: zeros, pipeline_mode=pl.Buffered(1))


def _mixer(h, moe, halo, lw, *, tl, n_real, pos0):
    bsz, seq, _ = h.shape
    nt = seq // tl
    n_tok = bsz * seq
    n_blocks = _num_blocks(bsz * nt * n_real)
    has_moe = moe is not None
    tile = pl.BlockSpec((1, tl, D_MODEL), lambda b, j, *_: (b, j, 0))
    smem = pl.BlockSpec(memory_space=pltpu.SMEM)
    consts = [halo[0], halo[1], halo[2], lw["norm_mix"], lw["w_in"], lw["conv_w"], lw["conv_b"],
              lw["conv_ln_g"], lw["conv_ln_b"], lw["w_conv_out"], lw["pool_w"], lw["pool_scale"],
              lw["sc_w"], lw["w_sc_out"], lw["w_out"], lw["norm_ffn"], lw["w_route"], lw["b_route"],
              jnp.tril(jnp.ones((tl, tl), jnp.bfloat16)),
              jnp.triu(jnp.ones((LANES, LANES), jnp.bfloat16), 1)]
    in_specs = ([tile] + ([pl.BlockSpec(memory_space=pl.ANY)] if has_moe else [])
                + [_const_spec(c.shape) for c in consts])
    out_shape = (
        jax.ShapeDtypeStruct((bsz, seq, D_MODEL), jnp.float32),
        jax.ShapeDtypeStruct((bsz, HALO_U, D_CONV), jnp.float32),
        jax.ShapeDtypeStruct((bsz, HALO_PB, D_POOL), jnp.float32),
        jax.ShapeDtypeStruct((bsz, HALO_CX, D_SC), jnp.float32),
        jax.ShapeDtypeStruct((n_blocks * BLK + tl, XS_W), jnp.float32),
        jax.ShapeDtypeStruct((bsz * nt, 8, tl), jnp.int32),
        jax.ShapeDtypeStruct((n_blocks + 1,), jnp.int32),
        jax.ShapeDtypeStruct((n_blocks + 1,), jnp.int32),
        jax.ShapeDtypeStruct((1,), jnp.int32),
    )
    out_specs = (
        tile,
        pl.BlockSpec((1, HALO_U, D_CONV), lambda b, j, *_: (b, 0, 0)),
        pl.BlockSpec((1, HALO_PB, D_POOL), lambda b, j, *_: (b, 0, 0)),
        pl.BlockSpec((1, HALO_CX, D_SC), lambda b, j, *_: (b, 0, 0)),
        pl.BlockSpec(memory_space=pl.ANY),
        pl.BlockSpec((1, 8, tl), lambda b, j, *_: (b * nt + j, 0, 0)),
        smem, smem, smem,
    )
    scratch = [
        pltpu.VMEM((HALO_U + tl + SUBLANES, D_CONV), jnp.float32),
        pltpu.VMEM((SUBLANES - 1, HALO_U + tl, D_CONV), jnp.float32),
        pltpu.VMEM((HALO_PB + tl, D_POOL), jnp.float32),
        pltpu.VMEM((HALO_CX + tl, D_SC), jnp.float32),
        pltpu.VMEM((tl, D_SC), jnp.float32),
        pltpu.VMEM((tl, 3 * D_MODEL), jnp.bfloat16),
        pltpu.VMEM((tl, D_CONV), jnp.bfloat16),
        pltpu.VMEM((tl, D_POOL), jnp.bfloat16),
        pltpu.VMEM((tl, D_SC), jnp.bfloat16),
        pltpu.VMEM((2, tl, XS_W), jnp.float32),
        pltpu.VMEM((8, tl + LANES), jnp.int32),
        pltpu.SMEM((tl + LANES,), jnp.int32),
        pltpu.VMEM((8, LANES), jnp.float32),
        pltpu.VMEM((8, LANES), jnp.int32),
        pltpu.SMEM((LANES,), jnp.int32),
        pltpu.SMEM((LANES,), jnp.int32),
        pltpu.SMEM((1,), jnp.int32),
        pltpu.SemaphoreType.DMA((6,)),
    ]
    if has_moe:
        scratch.append(pltpu.VMEM((2, tl, D_MODEL), jnp.float32))
    args = ([moe[0]] if has_moe else []) + [h] + ([moe[1]] if has_moe else []) + consts
    grid_spec = pltpu.PrefetchScalarGridSpec(
        num_scalar_prefetch=1 if has_moe else 0,
        grid=(bsz, nt), in_specs=in_specs, out_specs=out_specs, scratch_shapes=scratch)
    return pl.pallas_call(
        functools.partial(_mixer_kernel, tl=tl, n_real=n_real, pos0=pos0, has_moe=has_moe,
                          n_steps=bsz * nt, n_tok=n_tok, n_blocks=n_blocks),
        grid_spec=grid_spec,
        out_shape=out_shape,
        compiler_params=pltpu.CompilerParams(
            dimension_semantics=("arbitrary", "arbitrary"), vmem_limit_bytes=VMEM_LIMIT),
        name="mixer",
    )(*args)


def _experts_kernel(order_ref, be1_ref, be2_ref, nvalid_ref, nused_ref, xs_ref,
                    w1a_ref, w3a_ref, w2a_ref, w1b_ref, w3b_ref, w2b_ref, out_ref):
    i = pl.program_id(0)

    @pl.when(i < nused_ref[0])
    def _():
        rows = xs_ref[...]
        ok = lax.broadcasted_iota(jnp.int32, (BLK, 1), 0) < nvalid_ref[i]
        x = jnp.where(ok, rows[:, 0:D_MODEL], 0.0).astype(jnp.bfloat16)
        wa = jnp.where(ok, rows[:, LANE_WA:LANE_WA + 1], 0.0)
        wb = jnp.where(ok, rows[:, LANE_WB:LANE_WB + 1], 0.0)

        def expert(w1_ref, w3_ref, w2_ref):
            hid = jax.nn.silu(_dot(x, w1_ref[0])) * _dot(x, w3_ref[0])
            return _dot(hid.astype(jnp.bfloat16), w2_ref[0])

        out_ref[...] = (expert(w1a_ref, w3a_ref, w2a_ref) * wa
                        + expert(w1b_ref, w3b_ref, w2b_ref) * wb)


def _experts(xs, blk_combo, blk_nvalid, nused, lw):
    n_blocks = blk_combo.shape[0] - 1
    ids = jnp.arange(n_blocks, dtype=jnp.int32)
    fill_bits = BLK.bit_length() - 1
    id_bits = max(n_blocks - 1, 1).bit_length()
    key = ((blk_combo[:n_blocks] << (id_bits + fill_bits)) | (ids << fill_bits)
           | jnp.clip(blk_nvalid[:n_blocks] - 1, 0, BLK - 1))
    key = jnp.sort(key)
    last = lax.dynamic_slice(key, (jnp.maximum(nused[0] - 1, 0),), (1,))
    key = jnp.where(ids < nused[0], key, last)
    combo = key >> (id_bits + fill_bits)
    order = (key >> fill_bits) & ((1 << id_bits) - 1)
    nvalid = (key & (BLK - 1)) + 1
    grp = combo // N_PAIRS
    pair = combo % N_PAIRS
    a = jnp.sum(pair[:, None] >= jnp.asarray(_PAIR_OFF)[None, 1:], axis=1).astype(jnp.int32)
    off_a = a * (EPG - 1) - ((a * (a - 1)) >> 1)
    be1 = jnp.minimum(grp * EPG + a, N_EXPERTS - 1).astype(jnp.int32)
    be2 = jnp.minimum(grp * EPG + pair - off_a + a + 1, N_EXPERTS - 1).astype(jnp.int32)

    def wspec(which):
        def index_map(i, order_r, be1_r, be2_r, nv_r, nu_r):
            return ((be1_r, be2_r)[which][i], 0, 0)
        return index_map

    def blk_map(i, order_r, *_):
        return (order_r[i], 0)

    up = (1, D_MODEL, D_EXPERT)
    down = (1, D_EXPERT, D_MODEL)
    grid_spec = pltpu.PrefetchScalarGridSpec(
        num_scalar_prefetch=5,
        grid=(n_blocks,),
        in_specs=[
            pl.BlockSpec((BLK, XS_W), blk_map),
            pl.BlockSpec(up, wspec(0)), pl.BlockSpec(up, wspec(0)), pl.BlockSpec(down, wspec(0)),
            pl.BlockSpec(up, wspec(1)), pl.BlockSpec(up, wspec(1)), pl.BlockSpec(down, wspec(1)),
        ],
        out_specs=pl.BlockSpec((BLK, D_MODEL), blk_map),
    )
    return pl.pallas_call(
        _experts_kernel,
        grid_spec=grid_spec,
        out_shape=jax.ShapeDtypeStruct((n_blocks * BLK, D_MODEL), jnp.float32),
        compiler_params=pltpu.CompilerParams(
            dimension_semantics=("arbitrary",), vmem_limit_bytes=VMEM_LIMIT),
        name="experts",
    )(order.astype(jnp.int32), be1, be2, nvalid.astype(jnp.int32), nused, xs,
      lw["w1"], lw["w3"], lw["w2"], lw["w1"], lw["w3"], lw["w2"])


def _final_kernel(slot_ref, h_ref, moe_hbm, g_ref, out_ref, moe_buf, sem, *, rows, n_steps):
    i = pl.program_id(0)

    n_tok = rows * n_steps

    def row_copy(tile, r):
        t = jnp.minimum(tile * rows + r, n_tok - 1)
        return pltpu.make_async_copy(moe_hbm.at[pl.ds(slot_ref[t], 1), :],
                                     moe_buf.at[tile % 3, pl.ds(r, 1), :], sem.at[tile % 3])

    def wait_tile(tile):
        pltpu.make_async_copy(moe_hbm.at[pl.ds(0, rows), :], moe_buf.at[0], sem.at[tile % 3]).wait()

    @pl.when(i == 0)
    def _():
        def body(r, c):
            row_copy(0, r).start()
            row_copy(1, r).start()
            return c
        lax.fori_loop(0, rows, body, 0)
        wait_tile(0)

    for r in range(rows):
        row_copy(i + 2, r).start()
    out_ref[...] = _rms(h_ref[...] + moe_buf[i % 3], g_ref[...])
    wait_tile(i + 1)

    @pl.when(i == n_steps - 1)
    def _():
        wait_tile(i + 2)


def _final_norm(h, slot, moe_rows, g, rows):
    n = h.shape[0]
    tile = pl.BlockSpec((rows, D_MODEL), lambda i, *_: (i, 0))
    grid_spec = pltpu.PrefetchScalarGridSpec(
        num_scalar_prefetch=1,
        grid=(n // rows,),
        in_specs=[tile, pl.BlockSpec(memory_space=pl.ANY),
                  pl.BlockSpec((1, D_MODEL), lambda i, *_: (0, 0))],
        out_specs=tile,
        scratch_shapes=[pltpu.VMEM((3, rows, D_MODEL), jnp.float32), pltpu.SemaphoreType.DMA((3,))],
    )
    return pl.pallas_call(
        functools.partial(_final_kernel, rows=rows, n_steps=n // rows),
        grid_spec=grid_spec,
        out_shape=jax.ShapeDtypeStruct((n, D_MODEL), jnp.float32),
        compiler_params=pltpu.CompilerParams(
            dimension_semantics=("arbitrary",), vmem_limit_bytes=VMEM_LIMIT),
        name="final_norm",
    )(slot, h, moe_rows, g)


def _prep_kernel(w_ref, o_ref, *, halved):
    i = pl.program_id(0)
    scale = jnp.float32(1.0)
    for b in halved:
        scale = jnp.where(i == b, jnp.float32(0.5), scale)
    o_ref[0] = (w_ref[0] * scale).astype(jnp.bfloat16)


def _weight_blocks(w_stack, l, *, k_block, halved=()):
    _, k, n = w_stack.shape
    n_nb = n // W_BLOCK
    n_out = (k // k_block) * n_nb
    return pl.pallas_call(
        functools.partial(_prep_kernel, halved=tuple(halved)),
        grid=(n_out,),
        in_specs=[pl.BlockSpec((1, k_block, W_BLOCK), lambda i: (l, i // n_nb, i % n_nb))],
        out_specs=pl.BlockSpec((1, k_block, W_BLOCK), lambda i: (i, 0, 0)),
        out_shape=jax.ShapeDtypeStruct((n_out, k_block, W_BLOCK), jnp.bfloat16),
        compiler_params=pltpu.CompilerParams(dimension_semantics=("arbitrary",)),
        name="weight_blocks",
    )(w_stack)


def _cast_kernel(w_ref, o_ref):
    o_ref[...] = w_ref[0].astype(jnp.bfloat16)


def _expert_weights(w_stack, l):
    _, e, k, n = w_stack.shape
    per_step = 4
    return pl.pallas_call(
        _cast_kernel,
        grid=(e // per_step,),
        in_specs=[pl.BlockSpec((1, per_step, k, n), lambda i: (l, i, 0, 0))],
        out_specs=pl.BlockSpec((per_step, k, n), lambda i: (i, 0, 0)),
        out_shape=jax.ShapeDtypeStruct((e, k, n), jnp.bfloat16),
        compiler_params=pltpu.CompilerParams(
            dimension_semantics=("arbitrary",),
            vmem_limit_bytes=4 * per_step * k * n * (4 + 2)),
        name="expert_weights",
    )(w_stack)


def _layer_weights(p, l):
    bf = jnp.bfloat16
    w_route = jnp.zeros((D_MODEL, LANES), jnp.float32)
    w_route = w_route.at[:, :N_GROUPS].set(p["w_group"][l])
    w_route = w_route.at[:, ROUTE_E0:ROUTE_E0 + N_EXPERTS].set(p["w_router"][l])
    b_route = jnp.zeros((1, LANES), jnp.float32)
    b_route = b_route.at[0, :N_GROUPS].set(p["b_group"][l])
    b_route = b_route.at[0, ROUTE_E0:ROUTE_E0 + N_EXPERTS].set(p["b_router"][l])

    n_in_blocks = p["w_in"].shape[-1] // W_BLOCK
    halved_in = [b for b in range(n_in_blocks) if b * W_BLOCK < C_PB or b * W_BLOCK >= C_GA]
    n_wb = D_MODEL // W_BLOCK
    w_out = _weight_blocks(p["w_out"], l, k_block=W_BLOCK, halved=range(n_wb * n_wb))
    return {
        "norm_mix": p["norm_mix"][l][None, :],
        "w_in": _weight_blocks(p["w_in"], l, k_block=D_MODEL, halved=halved_in),
        "conv_w": jnp.broadcast_to(p["conv_w"][l][:, None, :], (CONV_K, SUBLANES, D_CONV)),
        "conv_b": p["conv_b"][l][None, :],
        "conv_ln_g": p["conv_ln_g"][l][None, :],
        "conv_ln_b": p["conv_ln_b"][l][None, :],
        "w_conv_out": _weight_blocks(p["w_conv_out"], l, k_block=D_CONV),
        "pool_w": p["pool_w"][l].astype(bf),
        "pool_scale": p["pool_scale"][l][None, :],
        "sc_w": p["sc_w"][l],
        "w_sc_out": _weight_blocks(p["w_sc_out"], l, k_block=D_SC),
        "w_out": w_out.reshape(n_wb, n_wb, W_BLOCK, W_BLOCK),
        "norm_ffn": p["norm_ffn"][l][None, :],
        "w_route": w_route.astype(bf),
        "b_route": b_route,
        "w1": _expert_weights(p["w1"], l),
        "w3": _expert_weights(p["w3"], l),
        "w2": _expert_weights(p["w2"], l),
    }


def _layer(h, moe, halo, lw, *, tl, n_real, pos0):
    h, su, spb, scx, xs, slot, blk_combo, blk_nvalid, nused = _mixer(
        h, moe, halo, lw, tl=tl, n_real=n_real, pos0=pos0)
    out = _experts(xs, blk_combo, blk_nvalid, nused, lw)
    return h, (slot[:, 0, :].reshape(-1), out), (su[0], spb[0], scx[0])


def kernel(x, meta, norm_mix, w_in, conv_w, conv_b, conv_ln_g, conv_ln_b, w_conv_out, pool_w,
           pool_scale, sc_w, w_sc_out, w_out, norm_ffn, w_group, b_group, w_router, b_router,
           w1, w3, w2, final_norm):
    p = dict(norm_mix=norm_mix, w_in=w_in, conv_w=conv_w, conv_b=conv_b, conv_ln_g=conv_ln_g,
             conv_ln_b=conv_ln_b, w_conv_out=w_conv_out, pool_w=pool_w, pool_scale=pool_scale,
             sc_w=sc_w, w_sc_out=w_sc_out, w_out=w_out, norm_ffn=norm_ffn, w_group=w_group,
             b_group=b_group, w_router=w_router, b_router=b_router, w1=w1, w3=w3, w2=w2)
    bsz, seq, _ = x.shape
    n_tok = bsz * seq
    n_layers = norm_mix.shape[0]
    lws = [_layer_weights(p, l) for l in range(n_layers)]
    zero_halo = (jnp.zeros((HALO_U, D_CONV), jnp.float32),
                 jnp.zeros((HALO_PB, D_POOL), jnp.float32),
                 jnp.zeros((HALO_CX, D_SC), jnp.float32))

    halos = []
    hm = jnp.zeros((1, META_TILE, D_MODEL), jnp.float32).at[0, :N_META].set(meta)
    moe_m = None
    for l, lw in enumerate(lws):
        if l + 1 < n_layers:
            hm, moe_m, halo = _layer(hm, moe_m, zero_halo, lw, tl=META_TILE, n_real=N_META, pos0=0)
        else:
            halo = _mixer(hm, moe_m, zero_halo, lw, tl=META_TILE, n_real=N_META, pos0=0)[1:4]
            halo = tuple(a[0] for a in halo)
        halos.append(halo)

    h = x
    moe = None
    for l, lw in enumerate(lws):
        h, moe, _ = _layer(h, moe, halos[l], lw, tl=MAIN_TILE, n_real=MAIN_TILE, pos0=N_META)
    out = _final_norm(h.reshape(n_tok, D_MODEL), moe[0], moe[1], final_norm[None, :], MAIN_TILE)
    return out.reshape(bsz, seq, D_MODEL)
```

```python
import functools

import numpy as np
import jax
import jax.numpy as jnp
from jax import lax
from jax.experimental import pallas as pl
from jax.experimental.pallas import tpu as pltpu

D_MODEL = 1024
N_META = 16
D_CONV = 512
CONV_K = 31
D_POOL = 512
POOL_WINDOWS = (2, 4, 8, 16)
POOL_GC = 128
POOL_OUT = 256
D_SC = 512
SC_K = 3
N_GROUPS = 4
EPG = 8
N_EXPERTS = 32
D_EXPERT = 256
EPS = 1e-6

C_ALIN, C_AGATE, C_PB, C_CB, C_CC, C_CX, C_GA, C_GB, C_GC = (
    0, 512, 1024, 1536, 2048, 2560, 3072, 4096, 5120)

LANES = 128
SUBLANES = 8
N_PAIRS = EPG * (EPG - 1) // 2
N_COMBO = N_GROUPS * N_PAIRS
ROUTE_E0 = 4
NO_COMBO = LANES - 1

HALO_U = 32
HALO_PB = 16
HALO_CX = 8
W_BLOCK = 512
GATE_BLOCK = W_BLOCK
ROW_CHUNK = 32

BLK = 256
XS_W = D_MODEL + LANES
LANE_WA, LANE_WB = D_MODEL + 1, D_MODEL + 2
MAIN_TILE = 512
META_TILE = 128
VMEM_LIMIT = 56 * 1024 * 1024

_PAIR_OFF = np.cumsum([0] + [EPG - 1 - a for a in range(EPG - 1)])[:-1].astype(np.int32)


def _sigmoid(x):
    return 0.5 * jnp.tanh(0.5 * x) + 0.5


def _rms(x, g):
    return x * lax.rsqrt(jnp.mean(x * x, axis=-1, keepdims=True) + EPS) * g


def _dot(a, b):
    return jnp.dot(a, b, preferred_element_type=jnp.float32)


def _num_blocks(n_tok):
    return n_tok // BLK + min(n_tok, N_COMBO)


def _route(logits):
    n = logits.shape[0]
    lane = lax.broadcasted_iota(jnp.int32, (n, LANES), 1)
    neg = jnp.float32(-jnp.inf)

    def first_max(vals):
        m = jnp.max(vals, axis=-1, keepdims=True)
        idx = jnp.min(jnp.where(vals == m, lane, LANES), axis=-1, keepdims=True)
        return m, idx

    is_g = lane < N_GROUPS
    lg = jnp.where(is_g, logits, neg)
    mg, g_sel = first_max(lg)
    p_sel = 1.0 / jnp.sum(jnp.where(is_g, jnp.exp(lg - mg), 0.0), axis=-1, keepdims=True)

    lo = ROUTE_E0 + EPG * g_sel
    le = jnp.where((lane >= lo) & (lane < lo + EPG), logits, neg)
    v1, i1 = first_max(le)
    v2, i2 = first_max(jnp.where(lane == i1, neg, le))
    t = jnp.exp(v2 - v1)
    w1 = p_sel / (1.0 + t)
    w2 = p_sel * t / (1.0 + t)

    e1 = i1 - lo
    e2 = i2 - lo
    a = jnp.minimum(e1, e2)
    b = jnp.maximum(e1, e2)
    wa = jnp.where(e1 < e2, w1, w2)
    wb = jnp.where(e1 < e2, w2, w1)
    pair = a * (EPG - 1) - ((a * (a - 1)) >> 1) + (b - a - 1)
    return g_sel * N_PAIRS + pair, wa, wb


def _mixer_kernel(*refs, tl, n_real, pos0, has_moe, n_steps, n_tok, n_blocks):
    it = iter(refs)
    slot_prev = next(it) if has_moe else None
    h_ref = next(it)
    moe_hbm = next(it) if has_moe else None
    (hu_ref, hpb_ref, hcx_ref, nmix_ref, w_in_ref, conv_w_ref, conv_b_ref, ln_g_ref, ln_b_ref,
     w_co_ref, pool_w_ref, pool_s_ref, sc_w_ref, w_so_ref, w_out_ref, nffn_ref, w_r_ref, b_r_ref,
     tri_ref, sup_ref,
     hout_ref, su_ref, spb_ref, scx_ref, xs_hbm, slot_ref, blk_combo, blk_nvalid, nused_ref,
     u_s, u_sh, pb_s, cx_s, cb_s, gate_s, act_a, act_b, act_c, stage, meta_v, meta_s, alloc,
     alloc_iv, cnt_s, cur_s, nf_s, sem) = [next(it) for _ in range(47)]
    moe_buf = next(it) if has_moe else None

    j = pl.program_id(1)
    s = pl.program_id(0) * pl.num_programs(1) + j
    dump_row0 = n_blocks * BLK
    SEM_SCATTER, SEM_META, SEM_FLUSH, SEM_GATHER = 0, 1, 2, 3
    ROW_CNT, ROW_CUR, ROW_NF = 0, 1, 2
    LANE_NNEW = LANES - 1

    def meta_copy():
        return pltpu.make_async_copy(meta_v.at[0], meta_s, sem.at[SEM_META])

    def gather_copy(tile, i):
        t = jnp.minimum(tile * tl + i, n_tok - 1)
        return pltpu.make_async_copy(
            moe_hbm.at[pl.ds(slot_prev[t], 1), :], moe_buf.at[tile % 2, pl.ds(i, 1), :], sem.at[SEM_GATHER])

    def gather_wait():
        pltpu.make_async_copy(moe_hbm.at[pl.ds(0, tl), :], moe_buf.at[0], sem.at[SEM_GATHER]).wait()

    def scatter_row(i, slot, par):
        pltpu.make_async_copy(stage.at[par, pl.ds(i, 1), :], xs_hbm.at[pl.ds(slot, 1), :],
                              sem.at[SEM_SCATTER]).start()

    def scatter_wait():
        pltpu.make_async_copy(stage.at[0], xs_hbm.at[pl.ds(0, tl), :], sem.at[SEM_SCATTER]).wait()

    def register_new_blocks():
        first = nf_s[0]
        n_new = meta_s[tl + LANE_NNEW]

        def body(o, c):
            blk_combo[first + o] = meta_s[tl + o]
            blk_nvalid[first + o] = BLK
            return c
        lax.fori_loop(0, n_new, body, 0)
        nf_s[0] = first + n_new

    @pl.when(s == 0)
    def _():
        def unused(i, c):
            blk_combo[i] = NO_COMBO
            blk_nvalid[i] = 0
            return c
        lax.fori_loop(0, n_blocks + 1, unused, 0)
        nf_s[0] = 0
        alloc[...] = jnp.zeros_like(alloc)
        meta_v[...] = jnp.zeros_like(meta_v)
        meta_copy().start()
        stage[...] = jnp.zeros_like(stage)
        u_s[HALO_U + tl:HALO_U + tl + SUBLANES, :] = jnp.zeros((SUBLANES, D_CONV), jnp.float32)
        if has_moe:
            def first(i, c):
                gather_copy(0, i).start()
                return c
            lax.fori_loop(0, tl, first, 0)
            gather_wait()

    @pl.when(j == 0)
    def _():
        u_s[0:HALO_U, :] = hu_ref[...]
        pb_s[0:HALO_PB, :] = hpb_ref[...]
        cx_s[0:HALO_CX, :] = hcx_ref[...]

    meta_copy().wait()
    register_new_blocks()

    prev_par = (s + 1) % 2
    cur_par = s % 2
    for i in range(tl):
        scatter_row(i, jnp.where(s > 0, meta_s[i], dump_row0 + i), prev_par)

    if has_moe:
        hin = h_ref[0] + moe_buf[cur_par]
        for i in range(tl):
            gather_copy(s + 1, i).start()
    else:
        hin = h_ref[0]
    xb = _rms(hin, nmix_ref[...]).astype(jnp.bfloat16)

    def proj(c0):
        return _dot(xb, w_in_ref[c0 // W_BLOCK])

    u_s[HALO_U:HALO_U + tl, :] = proj(C_ALIN) * (jnp.tanh(proj(C_AGATE)) + 1.0)
    pb_s[HALO_PB:HALO_PB + tl, :] = proj(C_PB)
    cx_s[HALO_CX:HALO_CX + tl, :] = proj(C_CC) * proj(C_CX)
    cb_s[...] = proj(C_CB)
    for lo in range(1, SUBLANES):
        u_sh[lo - 1] = u_s[lo:lo + HALO_U + tl, :]

    rc = min(ROW_CHUNK, tl)
    row0 = pos0 + j * tl

    n_chunks = tl // rc
    n_gate_blocks = 3 * D_MODEL // GATE_BLOCK

    for ci, base in enumerate(range(0, tl, rc)):
        for gb in range(ci * n_gate_blocks // n_chunks, (ci + 1) * n_gate_blocks // n_chunks):
            c0 = gb * GATE_BLOCK
            gate_s[:, c0:c0 + GATE_BLOCK] = (jnp.tanh(proj(C_GA + c0)) + 1.0).astype(jnp.bfloat16)
        acc = jnp.broadcast_to(conv_b_ref[...], (rc, D_CONV))
        for k in range(CONV_K):
            hi, lo = divmod(HALO_U - (CONV_K - 1) + k, SUBLANES)
            src = u_s if lo == 0 else u_sh.at[lo - 1]
            w_k = jnp.concatenate([conv_w_ref[k]] * (rc // SUBLANES), axis=0)
            acc = acc + w_k * src[pl.ds(base + SUBLANES * hi, rc), :]
        mu = jnp.mean(acc, axis=-1, keepdims=True)
        dev = acc - mu
        var = jnp.mean(dev * dev, axis=-1, keepdims=True)
        yn = dev * lax.rsqrt(var + EPS) * ln_g_ref[...] + ln_b_ref[...]
        act_a[pl.ds(base, rc), :] = (yn * _sigmoid(yn)).astype(jnp.bfloat16)
        for g, w in enumerate(POOL_WINDOWS):
            cols = slice(g * POOL_GC, (g + 1) * POOL_GC)
            v = pb_s[pl.ds(base + HALO_PB, rc), cols]
            sm = v
            for i in range(1, w):
                sm = sm + pb_s[pl.ds(base + (HALO_PB - i), rc), cols]
            if pos0 + 1 >= w:
                d = sm * (1.0 / w) - v
            else:
                pos = row0 + base + lax.broadcasted_iota(jnp.int32, (rc, POOL_GC), 0)
                d = sm / jnp.minimum(pos + 1, w).astype(jnp.float32) - v
            act_b[pl.ds(base, rc), cols] = d.astype(jnp.bfloat16)
        y = sc_w_ref[0:1, :] * cx_s[pl.ds(base + (HALO_CX - 2), rc), :]
        y = y + sc_w_ref[1:2, :] * cx_s[pl.ds(base + (HALO_CX - 1), rc), :]
        y = y + sc_w_ref[2:3, :] * cx_s[pl.ds(base + HALO_CX, rc), :]
        act_c[pl.ds(base, rc), :] = (cb_s[pl.ds(base, rc), :] * y).astype(jnp.bfloat16)

    new_u = u_s[n_real:n_real + HALO_U, :]
    new_pb = pb_s[n_real:n_real + HALO_PB, :]
    new_cx = cx_s[n_real:n_real + HALO_CX, :]
    u_s[0:HALO_U, :] = new_u
    pb_s[0:HALO_PB, :] = new_pb
    cx_s[0:HALO_CX, :] = new_cx
    su_ref[0] = new_u
    spb_ref[0] = new_pb
    scx_ref[0] = new_cx

    n_wb = D_MODEL // W_BLOCK
    groups_per_wb = W_BLOCK // POOL_OUT
    merged = []
    for nb in range(n_wb):
        cols = slice(nb * W_BLOCK, (nb + 1) * W_BLOCK)
        y_b = jnp.concatenate(
            [_dot(act_b[:, g * POOL_GC:(g + 1) * POOL_GC], pool_w_ref[g])
             for g in range(nb * groups_per_wb, (nb + 1) * groups_per_wb)], axis=-1) * pool_s_ref[:, cols]
        def gate(branch):
            c0 = branch * D_MODEL + nb * W_BLOCK
            return gate_s[:, c0:c0 + W_BLOCK].astype(jnp.float32)

        m = gate(0) * _dot(act_a[...], w_co_ref[nb])
        m = m + gate(1) * y_b
        m = m + gate(2) * _dot(act_c[...], w_so_ref[nb])
        merged.append(m.astype(jnp.bfloat16))
    hout = hin + jnp.concatenate(
        [sum(_dot(merged[kb], w_out_ref[kb, nb]) for kb in range(n_wb)) for nb in range(n_wb)], axis=-1)
    hout_ref[0] = hout

    xn = _rms(hout, nffn_ref[...])
    combo, wa, wb = _route(_dot(xn.astype(jnp.bfloat16), w_r_ref[...]) + b_r_ref[...])

    lane = lax.broadcasted_iota(jnp.int32, (tl, LANES), 1)
    onehot = lane == combo
    if n_real < tl:
        row_id = lax.broadcasted_iota(jnp.int32, (tl, 1), 0)
        onehot = onehot & (row_id < n_real)
    prefix = _dot(tri_ref[...], onehot.astype(jnp.bfloat16))
    n_tile = prefix[tl - 1:tl, :]
    cnt_v = alloc[ROW_CNT:ROW_CNT + 1, :]
    cur_v = alloc[ROW_CUR:ROW_CUR + 1, :]
    nf_v = alloc[ROW_NF:ROW_NF + 1, :]
    inv_blk = 1.0 / BLK

    def pick(vec):
        return jnp.sum(jnp.where(onehot, vec, 0.0), axis=-1, keepdims=True)

    rank = pick(prefix + cnt_v) - 1.0
    full_before = jnp.floor((cnt_v + (BLK - 1)) * inv_blk)
    n_new = jnp.floor((cnt_v + n_tile + (BLK - 1)) * inv_blk) - full_before
    new_excl = _dot(jnp.broadcast_to(n_new, (8, LANES)).astype(jnp.bfloat16), sup_ref[...])[0:1, :]
    new_base = nf_v + new_excl
    chunk = jnp.floor(rank * inv_blk)
    blk = jnp.where(chunk < pick(full_before), pick(cur_v), pick(new_base - full_before) + chunk)
    slot = blk * BLK + (rank - chunk * BLK)
    if n_real < tl:
        slot = jnp.where(row_id < n_real, slot, (dump_row0 + row_id).astype(jnp.float32))
    n_new_total = jnp.sum(n_new, axis=-1, keepdims=True)
    alloc[ROW_CNT:ROW_CNT + 1, :] = cnt_v + n_tile
    alloc[ROW_CUR:ROW_CUR + 1, :] = jnp.where(n_new > 0.0, new_base + n_new - 1.0, cur_v)
    alloc[ROW_NF:ROW_NF + 1, :] = nf_v + n_new_total

    sq_sub = lax.broadcasted_iota(jnp.int32, (LANES, LANES), 0)
    sq_lane = lax.broadcasted_iota(jnp.int32, (LANES, LANES), 1)
    new_incl_col = jnp.where(sq_sub == 0, jnp.broadcast_to(new_excl + n_new, (LANES, LANES)), 0.0).T[:, 0:1]
    new_combo = jnp.sum(jnp.where(new_incl_col <= sq_lane.astype(jnp.float32), 1.0, 0.0),
                        axis=0, keepdims=True)
    lane_row = lax.broadcasted_iota(jnp.int32, (1, LANES), 1)
    new_combo = jnp.where(lane_row == LANE_NNEW, n_new_total, new_combo)

    info = jnp.where(lane == 0, slot,
                     jnp.where(lane == LANE_WA - D_MODEL, wa,
                               jnp.where(lane == LANE_WB - D_MODEL, wb, 0.0)))
    scatter_wait()
    if has_moe:
        gather_wait()
    stage[cur_par, :, 0:D_MODEL] = xn
    stage[cur_par, :, D_MODEL:XS_W] = info
    slot_rows = info.T[0:8, :].astype(jnp.int32)
    if n_real < tl:
        slot_ref[0] = jnp.where(lax.broadcasted_iota(jnp.int32, (8, tl), 1) < n_real, slot_rows, 0)
    else:
        slot_ref[0] = slot_rows
    meta_v[:, 0:tl] = slot_rows
    meta_v[:, tl:tl + LANES] = jnp.broadcast_to(new_combo, (8, LANES)).astype(jnp.int32)
    meta_copy().start()

    @pl.when(s == n_steps - 1)
    def _():
        meta_copy().wait()
        register_new_blocks()

        def flush(i, c):
            scatter_row(i, meta_s[i], cur_par)
            return c
        lax.fori_loop(0, tl, flush, 0)
        scatter_wait()

        alloc_iv[...] = alloc[...].astype(jnp.int32)
        c0 = pltpu.make_async_copy(alloc_iv.at[ROW_CNT], cnt_s, sem.at[SEM_FLUSH])
        c1 = pltpu.make_async_copy(alloc_iv.at[ROW_CUR], cur_s, sem.at[SEM_FLUSH])
        c0.start()
        c1.start()
        c0.wait()
        c1.wait()

        def close(c, carry):
            n = cnt_s[c]
            bidx = jnp.where(n > 0, cur_s[c], n_blocks)
            blk_nvalid[bidx] = ((n - 1) & (BLK - 1)) + 1
            return carry
        lax.fori_loop(0, LANES, close, 0)
        nused_ref[0] = nf_s[0]


def _const_spec(shape):
    zeros = (0,) * len(shape)
    return pl.BlockSpec(shape, lambda b, j, *_: zeros, pipeline_mode=pl.Buffered(1))


def _mixer(h, moe, halo, lw, *, tl, n_real, pos0):
    bsz, seq, _ = h.shape
    nt = seq // tl
    n_tok = bsz * seq
    n_blocks = _num_blocks(bsz * nt * n_real)
    has_moe = moe is not None
    tile = pl.BlockSpec((1, tl, D_MODEL), lambda b, j, *_: (b, j, 0))
    smem = pl.BlockSpec(memory_space=pltpu.SMEM)
    consts = [halo[0], halo[1], halo[2], lw["norm_mix"], lw["w_in"], lw["conv_w"], lw["conv_b"],
              lw["conv_ln_g"], lw["conv_ln_b"], lw["w_conv_out"], lw["pool_w"], lw["pool_scale"],
              lw["sc_w"], lw["w_sc_out"], lw["w_out"], lw["norm_ffn"], lw["w_route"], lw["b_route"],
              jnp.tril(jnp.ones((tl, tl), jnp.bfloat16)),
              jnp.triu(jnp.ones((LANES, LANES), jnp.bfloat16), 1)]
    in_specs = ([tile] + ([pl.BlockSpec(memory_space=pl.ANY)] if has_moe else [])
                + [_const_spec(c.shape) for c in consts])
    out_shape = (
        jax.ShapeDtypeStruct((bsz, seq, D_MODEL), jnp.float32),
        jax.ShapeDtypeStruct((bsz, HALO_U, D_CONV), jnp.float32),
        jax.ShapeDtypeStruct((bsz, HALO_PB, D_POOL), jnp.float32),
        jax.ShapeDtypeStruct((bsz, HALO_CX, D_SC), jnp.float32),
        jax.ShapeDtypeStruct((n_blocks * BLK + tl, XS_W), jnp.float32),
        jax.ShapeDtypeStruct((bsz * nt, 8, tl), jnp.int32),
        jax.ShapeDtypeStruct((n_blocks + 1,), jnp.int32),
        jax.ShapeDtypeStruct((n_blocks + 1,), jnp.int32),
        jax.ShapeDtypeStruct((1,), jnp.int32),
    )
    out_specs = (
        tile,
        pl.BlockSpec((1, HALO_U, D_CONV), lambda b, j, *_: (b, 0, 0)),
        pl.BlockSpec((1, HALO_PB, D_POOL), lambda b, j, *_: (b, 0, 0)),
        pl.BlockSpec((1, HALO_CX, D_SC), lambda b, j, *_: (b, 0, 0)),
        pl.BlockSpec(memory_space=pl.ANY),
        pl.BlockSpec((1, 8, tl), lambda b, j, *_: (b * nt + j, 0, 0)),
        smem, smem, smem,
    )
    scratch = [
        pltpu.VMEM((HALO_U + tl + SUBLANES, D_CONV), jnp.float32),
        pltpu.VMEM((SUBLANES - 1, HALO_U + tl, D_CONV), jnp.float32),
        pltpu.VMEM((HALO_PB + tl, D_POOL), jnp.float32),
        pltpu.VMEM((HALO_CX + tl, D_SC), jnp.float32),
        pltpu.VMEM((tl, D_SC), jnp.float32),
        pltpu.VMEM((tl, 3 * D_MODEL), jnp.bfloat16),
        pltpu.VMEM((tl, D_CONV), jnp.bfloat16),
        pltpu.VMEM((tl, D_POOL), jnp.bfloat16),
        pltpu.VMEM((tl, D_SC), jnp.bfloat16),
        pltpu.VMEM((2, tl, XS_W), jnp.float32),
        pltpu.VMEM((8, tl + LANES), jnp.int32),
        pltpu.SMEM((tl + LANES,), jnp.int32),
        pltpu.VMEM((8, LANES), jnp.float32),
        pltpu.VMEM((8, LANES), jnp.int32),
        pltpu.SMEM((LANES,), jnp.int32),
        pltpu.SMEM((LANES,), jnp.int32),
        pltpu.SMEM((1,), jnp.int32),
        pltpu.SemaphoreType.DMA((6,)),
    ]
    if has_moe:
        scratch.append(pltpu.VMEM((2, tl, D_MODEL), jnp.float32))
    args = ([moe[0]] if has_moe else []) + [h] + ([moe[1]] if has_moe else []) + consts
    grid_spec = pltpu.PrefetchScalarGridSpec(
        num_scalar_prefetch=1 if has_moe else 0,
        grid=(bsz, nt), in_specs=in_specs, out_specs=out_specs, scratch_shapes=scratch)
    return pl.pallas_call(
        functools.partial(_mixer_kernel, tl=tl, n_real=n_real, pos0=pos0, has_moe=has_moe,
                          n_steps=bsz * nt, n_tok=n_tok, n_blocks=n_blocks),
        grid_spec=grid_spec,
        out_shape=out_shape,
        compiler_params=pltpu.CompilerParams(
            dimension_semantics=("arbitrary", "arbitrary"), vmem_limit_bytes=VMEM_LIMIT),
        name="mixer",
    )(*args)


def _experts_kernel(order_ref, be1_ref, be2_ref, nvalid_ref, nused_ref, xs_ref,
                    w1a_ref, w3a_ref, w2a_ref, w1b_ref, w3b_ref, w2b_ref, out_ref):
    i = pl.program_id(0)

    @pl.when(i < nused_ref[0])
    def _():
        rows = xs_ref[...]
        ok = lax.broadcasted_iota(jnp.int32, (BLK, 1), 0) < nvalid_ref[i]
        x = jnp.where(ok, rows[:, 0:D_MODEL], 0.0).astype(jnp.bfloat16)
        wa = jnp.where(ok, rows[:, LANE_WA:LANE_WA + 1], 0.0)
        wb = jnp.where(ok, rows[:, LANE_WB:LANE_WB + 1], 0.0)

        up = [(_dot(x, w1_ref[0]), _dot(x, w3_ref[0])) for w1_ref, w3_ref in ((w1a_ref, w3a_ref),
                                                                               (w1b_ref, w3b_ref))]
        hid = [(jax.nn.silu(u1) * u3).astype(jnp.bfloat16) for u1, u3 in up]
        out_ref[...] = _dot(hid[0], w2a_ref[0]) * wa + _dot(hid[1], w2b_ref[0]) * wb


def _experts(xs, blk_combo, blk_nvalid, nused, lw):
    n_blocks = blk_combo.shape[0] - 1
    ids = jnp.arange(n_blocks, dtype=jnp.int32)
    fill_bits = BLK.bit_length() - 1
    id_bits = max(n_blocks - 1, 1).bit_length()
    key = ((blk_combo[:n_blocks] << (id_bits + fill_bits)) | (ids << fill_bits)
           | jnp.clip(blk_nvalid[:n_blocks] - 1, 0, BLK - 1))
    key = jnp.sort(key)
    last = lax.dynamic_slice(key, (jnp.maximum(nused[0] - 1, 0),), (1,))
    key = jnp.where(ids < nused[0], key, last)
    combo = key >> (id_bits + fill_bits)
    order = (key >> fill_bits) & ((1 << id_bits) - 1)
    nvalid = (key & (BLK - 1)) + 1
    grp = combo // N_PAIRS
    pair = combo % N_PAIRS
    a = jnp.sum(pair[:, None] >= jnp.asarray(_PAIR_OFF)[None, 1:], axis=1).astype(jnp.int32)
    off_a = a * (EPG - 1) - ((a * (a - 1)) >> 1)
    be1 = jnp.minimum(grp * EPG + a, N_EXPERTS - 1).astype(jnp.int32)
    be2 = jnp.minimum(grp * EPG + pair - off_a + a + 1, N_EXPERTS - 1).astype(jnp.int32)

    def wspec(which):
        def index_map(i, order_r, be1_r, be2_r, nv_r, nu_r):
            return ((be1_r, be2_r)[which][i], 0, 0)
        return index_map

    def blk_map(i, order_r, *_):
        return (order_r[i], 0)

    up = (1, D_MODEL, D_EXPERT)
    down = (1, D_EXPERT, D_MODEL)
    grid_spec = pltpu.PrefetchScalarGridSpec(
        num_scalar_prefetch=5,
        grid=(n_blocks,),
        in_specs=[
            pl.BlockSpec((BLK, XS_W), blk_map),
            pl.BlockSpec(up, wspec(0)), pl.BlockSpec(up, wspec(0)), pl.BlockSpec(down, wspec(0)),
            pl.BlockSpec(up, wspec(1)), pl.BlockSpec(up, wspec(1)), pl.BlockSpec(down, wspec(1)),
        ],
        out_specs=pl.BlockSpec((BLK, D_MODEL), blk_map),
    )
    return pl.pallas_call(
        _experts_kernel,
        grid_spec=grid_spec,
        out_shape=jax.ShapeDtypeStruct((n_blocks * BLK, D_MODEL), jnp.float32),
        compiler_params=pltpu.CompilerParams(
            dimension_semantics=("arbitrary",), vmem_limit_bytes=VMEM_LIMIT),
        name="experts",
    )(order.astype(jnp.int32), be1, be2, nvalid.astype(jnp.int32), nused, xs,
      lw["w1"], lw["w3"], lw["w2"], lw["w1"], lw["w3"], lw["w2"])


def _final_kernel(slot_ref, h_ref, moe_hbm, g_ref, out_ref, moe_buf, sem, *, rows, n_steps):
    i = pl.program_id(0)

    n_tok = rows * n_steps

    def row_copy(tile, r):
        t = jnp.minimum(tile * rows + r, n_tok - 1)
        return pltpu.make_async_copy(moe_hbm.at[pl.ds(slot_ref[t], 1), :],
                                     moe_buf.at[tile % 3, pl.ds(r, 1), :], sem.at[tile % 3])

    def wait_tile(tile):
        pltpu.make_async_copy(moe_hbm.at[pl.ds(0, rows), :], moe_buf.at[0], sem.at[tile % 3]).wait()

    @pl.when(i == 0)
    def _():
        def body(r, c):
            row_copy(0, r).start()
            row_copy(1, r).start()
            return c
        lax.fori_loop(0, rows, body, 0)
        wait_tile(0)

    for r in range(rows):
        row_copy(i + 2, r).start()
    out_ref[...] = _rms(h_ref[...] + moe_buf[i % 3], g_ref[...])
    wait_tile(i + 1)

    @pl.when(i == n_steps - 1)
    def _():
        wait_tile(i + 2)


def _final_norm(h, slot, moe_rows, g, rows):
    n = h.shape[0]
    tile = pl.BlockSpec((rows, D_MODEL), lambda i, *_: (i, 0))
    grid_spec = pltpu.PrefetchScalarGridSpec(
        num_scalar_prefetch=1,
        grid=(n // rows,),
        in_specs=[tile, pl.BlockSpec(memory_space=pl.ANY),
                  pl.BlockSpec((1, D_MODEL), lambda i, *_: (0, 0))],
        out_specs=tile,
        scratch_shapes=[pltpu.VMEM((3, rows, D_MODEL), jnp.float32), pltpu.SemaphoreType.DMA((3,))],
    )
    return pl.pallas_call(
        functools.partial(_final_kernel, rows=rows, n_steps=n // rows),
        grid_spec=grid_spec,
        out_shape=jax.ShapeDtypeStruct((n, D_MODEL), jnp.float32),
        compiler_params=pltpu.CompilerParams(
            dimension_semantics=("arbitrary",), vmem_limit_bytes=VMEM_LIMIT),
        name="final_norm",
    )(slot, h, moe_rows, g)


def _prep_kernel(w_ref, o_ref, *, halved):
    i = pl.program_id(0)
    scale = jnp.float32(1.0)
    for b in halved:
        scale = jnp.where(i == b, jnp.float32(0.5), scale)
    o_ref[0] = (w_ref[0] * scale).astype(jnp.bfloat16)


def _weight_blocks(w_stack, l, *, k_block, halved=()):
    _, k, n = w_stack.shape
    n_nb = n // W_BLOCK
    n_out = (k // k_block) * n_nb
    return pl.pallas_call(
        functools.partial(_prep_kernel, halved=tuple(halved)),
        grid=(n_out,),
        in_specs=[pl.BlockSpec((1, k_block, W_BLOCK), lambda i: (l, i // n_nb, i % n_nb))],
        out_specs=pl.BlockSpec((1, k_block, W_BLOCK), lambda i: (i, 0, 0)),
        out_shape=jax.ShapeDtypeStruct((n_out, k_block, W_BLOCK), jnp.bfloat16),
        compiler_params=pltpu.CompilerParams(dimension_semantics=("arbitrary",)),
        name="weight_blocks",
    )(w_stack)


def _cast_kernel(w_ref, o_ref):
    o_ref[...] = w_ref[0].astype(jnp.bfloat16)


def _expert_weights(w_stack, l):
    _, e, k, n = w_stack.shape
    per_step = 4
    return pl.pallas_call(
        _cast_kernel,
        grid=(e // per_step,),
        in_specs=[pl.BlockSpec((1, per_step, k, n), lambda i: (l, i, 0, 0))],
        out_specs=pl.BlockSpec((per_step, k, n), lambda i: (i, 0, 0)),
        out_shape=jax.ShapeDtypeStruct((e, k, n), jnp.bfloat16),
        compiler_params=pltpu.CompilerParams(
            dimension_semantics=("arbitrary",),
            vmem_limit_bytes=4 * per_step * k * n * (4 + 2)),
        name="expert_weights",
    )(w_stack)


def _layer_weights(p, l):
    bf = jnp.bfloat16
    w_route = jnp.zeros((D_MODEL, LANES), jnp.float32)
    w_route = w_route.at[:, :N_GROUPS].set(p["w_group"][l])
    w_route = w_route.at[:, ROUTE_E0:ROUTE_E0 + N_EXPERTS].set(p["w_router"][l])
    b_route = jnp.zeros((1, LANES), jnp.float32)
    b_route = b_route.at[0, :N_GROUPS].set(p["b_group"][l])
    b_route = b_route.at[0, ROUTE_E0:ROUTE_E0 + N_EXPERTS].set(p["b_router"][l])

    n_in_blocks = p["w_in"].shape[-1] // W_BLOCK
    halved_in = [b for b in range(n_in_blocks) if b * W_BLOCK < C_PB or b * W_BLOCK >= C_GA]
    n_wb = D_MODEL // W_BLOCK
    w_out = _weight_blocks(p["w_out"], l, k_block=W_BLOCK, halved=range(n_wb * n_wb))
    return {
        "norm_mix": p["norm_mix"][l][None, :],
        "w_in": _weight_blocks(p["w_in"], l, k_block=D_MODEL, halved=halved_in),
        "conv_w": jnp.broadcast_to(p["conv_w"][l][:, None, :], (CONV_K, SUBLANES, D_CONV)),
        "conv_b": p["conv_b"][l][None, :],
        "conv_ln_g": p["conv_ln_g"][l][None, :],
        "conv_ln_b": p["conv_ln_b"][l][None, :],
        "w_conv_out": _weight_blocks(p["w_conv_out"], l, k_block=D_CONV),
        "pool_w": p["pool_w"][l].astype(bf),
        "pool_scale": p["pool_scale"][l][None, :],
        "sc_w": p["sc_w"][l],
        "w_sc_out": _weight_blocks(p["w_sc_out"], l, k_block=D_SC),
        "w_out": w_out.reshape(n_wb, n_wb, W_BLOCK, W_BLOCK),
        "norm_ffn": p["norm_ffn"][l][None, :],
        "w_route": w_route.astype(bf),
        "b_route": b_route,
        "w1": _expert_weights(p["w1"], l),
        "w3": _expert_weights(p["w3"], l),
        "w2": _expert_weights(p["w2"], l),
    }


def _layer(h, moe, halo, lw, *, tl, n_real, pos0):
    h, su, spb, scx, xs, slot, blk_combo, blk_nvalid, nused = _mixer(
        h, moe, halo, lw, tl=tl, n_real=n_real, pos0=pos0)
    out = _experts(xs, blk_combo, blk_nvalid, nused, lw)
    return h, (slot[:, 0, :].reshape(-1), out), (su[0], spb[0], scx[0])


def kernel(x, meta, norm_mix, w_in, conv_w, conv_b, conv_ln_g, conv_ln_b, w_conv_out, pool_w,
           pool_scale, sc_w, w_sc_out, w_out, norm_ffn, w_group, b_group, w_router, b_router,
           w1, w3, w2, final_norm):
    p = dict(norm_mix=norm_mix, w_in=w_in, conv_w=conv_w, conv_b=conv_b, conv_ln_g=conv_ln_g,
             conv_ln_b=conv_ln_b, w_conv_out=w_conv_out, pool_w=pool_w, pool_scale=pool_scale,
             sc_w=sc_w, w_sc_out=w_sc_out, w_out=w_out, norm_ffn=norm_ffn, w_group=w_group,
             b_group=b_group, w_router=w_router, b_router=b_router, w1=w1, w3=w3, w2=w2)
    bsz, seq, _ = x.shape
    n_tok = bsz * seq
    n_layers = norm_mix.shape[0]
    lws = [_layer_weights(p, l) for l in range(n_layers)]
    zero_halo = (jnp.zeros((HALO_U, D_CONV), jnp.float32),
                 jnp.zeros((HALO_PB, D_POOL), jnp.float32),
                 jnp.zeros((HALO_CX, D_SC), jnp.float32))

    halos = []
    hm = jnp.zeros((1, META_TILE, D_MODEL), jnp.float32).at[0, :N_META].set(meta)
    moe_m = None
    for l, lw in enumerate(lws):
        if l + 1 < n_layers:
            hm, moe_m, halo = _layer(hm, moe_m, zero_halo, lw, tl=META_TILE, n_real=N_META, pos0=0)
        else:
            halo = _mixer(hm, moe_m, zero_halo, lw, tl=META_TILE, n_real=N_META, pos0=0)[1:4]
            halo = tuple(a[0] for a in halo)
        halos.append(halo)

    h = x
    moe = None
    for l, lw in enumerate(lws):
        h, moe, _ = _layer(h, moe, halos[l], lw, tl=MAIN_TILE, n_real=MAIN_TILE, pos0=N_META)
    out = _final_norm(h.reshape(n_tok, D_MODEL), moe[0], moe[1], final_norm[None, :], MAIN_TILE)
    return out.reshape(bsz, seq, D_MODEL)
```
